```python
import math
import jax, jax.numpy as jnp
from jax import lax
import numpy as np

D_MODEL = 1024
BATCH = 4
SEQ = 8192
DEPTH = 1

HEAD_DIM = 64
D_MIX = D_MODEL
SWA_WIDTH = D_MIX // 2
MOBA_WIDTH = D_MIX - SWA_WIDTH
SWA_Q_HEADS = SWA_WIDTH // HEAD_DIM
SWA_KV_HEADS = 2
SWA_KV_WIDTH = SWA_KV_HEADS * HEAD_DIM
MOBA_HEADS = MOBA_WIDTH // HEAD_DIM
N_HEADS_TOTAL = SWA_Q_HEADS + MOBA_HEADS
WINDOW = 128
SWA_BLOCK = 128
MOBA_BLOCK = 256
MOBA_TOPK = 3
MOBA_QCHUNK = 32
NUM_BUCKETS = 32
MAX_DISTANCE = 1024
PLE_DIM = 256
RMS_EPS = 1e-6
IN_SPLITS = (SWA_WIDTH, SWA_KV_WIDTH, SWA_KV_WIDTH, SWA_WIDTH,
             MOBA_WIDTH, MOBA_WIDTH, MOBA_WIDTH, MOBA_WIDTH)
IN_COLS = sum(IN_SPLITS)

kernel_name = "hybrid_swa_sink_moba_t5bias_ple"


def rms_norm(x, g):
    xf = x.astype(jnp.float32)
    r = lax.rsqrt(jnp.mean(xf * xf, axis=-1, keepdims=True) + RMS_EPS)
    return (xf * r).astype(x.dtype) * g


def rel_bucket(dist):
    n = jnp.maximum(dist, 0)
    max_exact = NUM_BUCKETS // 2
    nf = jnp.maximum(n, 1).astype(jnp.float32)
    large = max_exact + (jnp.log(nf / max_exact) / math.log(MAX_DISTANCE / max_exact)
                         * (NUM_BUCKETS - max_exact)).astype(jnp.int32)
    large = jnp.minimum(large, NUM_BUCKETS - 1)
    return jnp.where(n < max_exact, n, large)


def swa_attention(q, k, v, sinks, bias_table):
    B, S, Hq, Dh = q.shape
    Hkv = k.shape[2]
    G = Hq // Hkv
    blk = SWA_BLOCK
    nb = S // blk
    qb = q.reshape(B, nb, blk, Hkv, G, Dh)
    pad = ((0, 0), (blk, 0), (0, 0), (0, 0))
    kb = jnp.pad(k, pad).reshape(B, nb + 1, blk, Hkv, Dh)
    vb = jnp.pad(v, pad).reshape(B, nb + 1, blk, Hkv, Dh)
    kband = jnp.concatenate([kb[:, :-1], kb[:, 1:]], axis=2)
    vband = jnp.concatenate([vb[:, :-1], vb[:, 1:]], axis=2)
    logits = jnp.einsum('bnqhgd,bnkhd->bnhgqk', qb, kband).astype(jnp.float32) * (Dh ** -0.5)
    qi = jnp.arange(blk, dtype=jnp.int32)[:, None]
    kj = jnp.arange(2 * blk, dtype=jnp.int32)[None, :]
    dist = qi + blk - kj
    bias = bias_table[rel_bucket(dist)].astype(jnp.float32)
    bias = bias.reshape(blk, 2 * blk, Hkv, G).transpose(2, 3, 0, 1)
    kpos = jnp.arange(nb, dtype=jnp.int32)[:, None] * blk - blk + kj
    valid = ((dist >= 0) & (dist < WINDOW))[None] & (kpos >= 0)[:, None, :]
    logits = jnp.where(valid[None, :, None, None], logits + bias, -jnp.inf)
    sink = sinks.astype(jnp.float32).reshape(Hkv, G)[None, None, :, :, None, None]
    m = jnp.maximum(jnp.max(logits, axis=-1, keepdims=True), sink)
    e = jnp.exp(logits - m)
    probs = e / (jnp.sum(e, axis=-1, keepdims=True) + jnp.exp(sink - m))
    out = jnp.einsum('bnhgqk,bnkhd->bnqhgd', probs.astype(v.dtype), vband)
    return out.reshape(B, S, Hq * Dh)


def moba_attention(q, k, v, bias_table):
    B, S, H, Dh = q.shape
    BS = MOBA_BLOCK
    nblk = -(-S // BS)
    Sp = nblk * BS
    pad = ((0, 0), (0, Sp - S), (0, 0), (0, 0))
    q, k, v = jnp.pad(q, pad), jnp.pad(k, pad), jnp.pad(v, pad)
    kblocks = k.reshape(B, nblk, BS, H, Dh)
    vblocks = v.reshape(B, nblk, BS, H, Dh)
    kmean = jnp.mean(kblocks.astype(jnp.float32), axis=2)
    scores = jnp.einsum('bshd,bnhd->bshn', q.astype(jnp.float32), kmean)
    qblk = jnp.arange(Sp, dtype=jnp.int32) // BS
    past = jnp.arange(nblk, dtype=jnp.int32)[None, :] < qblk[:, None]
    scores = jnp.where(past[None, :, None, :], scores, -jnp.inf)
    k_sel = max(1, min(MOBA_TOPK, nblk))
    _, idx = lax.top_k(scores, k_sel)
    valid = idx < qblk[None, :, None, None]

    kbh = kblocks.transpose(0, 3, 1, 2, 4)
    vbh = vblocks.transpose(0, 3, 1, 2, 4)
    QC = MOBA_QCHUNK
    nch = Sp // QC
    q_c = q.reshape(B, nch, QC, H, Dh).transpose(1, 0, 2, 3, 4)
    idx_c = idx.reshape(B, nch, QC, H, k_sel).transpose(1, 0, 2, 3, 4)
    valid_c = valid.reshape(B, nch, QC, H, k_sel).transpose(1, 0, 2, 3, 4)
    bi = jnp.arange(B)[:, None, None, None]
    hi = jnp.arange(H)[None, None, :, None]
    hi5 = jnp.arange(H)[None, None, :, None, None]
    key_off = jnp.arange(BS, dtype=jnp.int32)
    scale = Dh ** -0.5

    def chunk_fn(args):
        qc, idxc, validc, c = args
        ksel = kbh[bi, hi, idxc]
        vsel = vbh[bi, hi, idxc]
        qpos = c * QC + jnp.arange(QC, dtype=jnp.int32)
        logit_sel = jnp.einsum('bqhd,bqhskd->bqhsk', qc, ksel).astype(jnp.float32) * scale
        kpos_sel = idxc[..., None] * BS + key_off
        dist_sel = qpos[None, :, None, None, None] - kpos_sel
        bias_sel = bias_table[rel_bucket(dist_sel), hi5].astype(jnp.float32)
        logit_sel = jnp.where(validc[..., None], logit_sel + bias_sel, -jnp.inf)
        own = (c * QC) // BS
        kown = lax.dynamic_slice_in_dim(k, own * BS, BS, axis=1)
        vown = lax.dynamic_slice_in_dim(v, own * BS, BS, axis=1)
        logit_own = jnp.einsum('bqhd,bkhd->bqhk', qc, kown).astype(jnp.float32) * scale
        dist_own = qpos[:, None] - (own * BS + key_off)[None, :]
        bias_own = bias_table[rel_bucket(dist_own)].astype(jnp.float32).transpose(0, 2, 1)
        logit_own = jnp.where((dist_own >= 0)[None, :, None, :], logit_own + bias_own[None], -jnp.inf)
        logits = jnp.concatenate([logit_sel.reshape(B, QC, H, k_sel * BS), logit_own], axis=-1)
        probs = jax.nn.softmax(logits, axis=-1).astype(v.dtype)
        p_sel = probs[..., :k_sel * BS].reshape(B, QC, H, k_sel, BS)
        p_own = probs[..., k_sel * BS:]
        return (jnp.einsum('bqhsk,bqhskd->bqhd', p_sel, vsel)
                + jnp.einsum('bqhk,bkhd->bqhd', p_own, vown))

    out = lax.map(chunk_fn, (q_c, idx_c, valid_c, jnp.arange(nch, dtype=jnp.int32)))
    out = out.transpose(1, 0, 2, 3, 4).reshape(B, Sp, H * Dh)
    return out[:, :S]


def hybrid_layer(x, p_i, norm_g, w_in, sinks, rel_bias, w_out, ple_g, w_ple_gate, w_ple_proj):
    B, S, _ = x.shape
    h = rms_norm(x, norm_g)
    proj = h @ w_in
    offs = np.cumsum(IN_SPLITS)[:-1].tolist()
    a_q, a_k, a_v, a_g, b_q, b_k, b_v, b_g = jnp.split(proj, offs, axis=-1)
    a_out = swa_attention(a_q.reshape(B, S, SWA_Q_HEADS, HEAD_DIM),
                          a_k.reshape(B, S, SWA_KV_HEADS, HEAD_DIM),
                          a_v.reshape(B, S, SWA_KV_HEADS, HEAD_DIM),
                          sinks, rel_bias[:, :SWA_Q_HEADS])
    b_out = moba_attention(b_q.reshape(B, S, MOBA_HEADS, HEAD_DIM),
                           b_k.reshape(B, S, MOBA_HEADS, HEAD_DIM),
                           b_v.reshape(B, S, MOBA_HEADS, HEAD_DIM),
                           rel_bias[:, SWA_Q_HEADS:])
    mixed = jnp.concatenate([a_out * jax.nn.silu(a_g), b_out * jax.nn.silu(b_g)], axis=-1)
    x = x + mixed @ w_out
    gate = jax.nn.sigmoid(rms_norm(x, ple_g) @ w_ple_gate)
    return x + (p_i @ w_ple_proj) * gate


def setup_inputs(seed: int = 0) -> dict:
    key = jax.random.key(seed)
    ks = jax.random.split(key, 12)
    f32 = jnp.float32
    return {
        "x": jax.random.normal(ks[0], (BATCH, SEQ, D_MODEL), f32),
        "p": jax.random.normal(ks[1], (DEPTH, BATCH, SEQ, PLE_DIM), f32),
        "norm_in": 1.0 + 0.02 * jax.random.normal(ks[2], (DEPTH, D_MODEL), f32),
        "w_in": jax.random.normal(ks[3], (DEPTH, D_MODEL, IN_COLS), f32) * D_MODEL ** -0.5,
        "sinks": 0.5 * jax.random.normal(ks[4], (DEPTH, SWA_Q_HEADS), f32),
        "rel_bias": 0.5 * jax.random.normal(ks[5], (NUM_BUCKETS, N_HEADS_TOTAL), f32),
        "w_out": jax.random.normal(ks[6], (DEPTH, D_MIX, D_MODEL), f32) * D_MIX ** -0.5,
        "ple_norm": 1.0 + 0.02 * jax.random.normal(ks[7], (DEPTH, D_MODEL), f32),
        "w_ple_gate": jax.random.normal(ks[8], (DEPTH, D_MODEL, D_MODEL), f32) * D_MODEL ** -0.5,
        "w_ple_proj": jax.random.normal(ks[9], (DEPTH, PLE_DIM, D_MODEL), f32) * (0.5 * PLE_DIM ** -0.5),
        "final_norm": 1.0 + 0.02 * jax.random.normal(ks[10], (D_MODEL,), f32),
    }


def reference(x, p, norm_in, w_in, sinks, rel_bias, w_out, ple_norm, w_ple_gate, w_ple_proj, final_norm):
    for i in range(DEPTH):
        x = hybrid_layer(x, p[i], norm_in[i], w_in[i], sinks[i], rel_bias, w_out[i],
                         ple_norm[i], w_ple_gate[i], w_ple_proj[i])
    return rms_norm(x, final_norm)
```

```python
import functools
import math

import jax
import jax.numpy as jnp
from jax import lax
from jax.experimental import pallas as pl
from jax.experimental.pallas import tpu as pltpu

F32 = jnp.float32
BF16 = jnp.bfloat16

HEAD_DIM = 64
SWA_Q_HEADS = 8
SWA_KV_HEADS = 2
MOBA_HEADS = 8
WINDOW = 128
MOBA_BLOCK = 256
MOBA_TOPK = 3
NUM_BUCKETS = 32
MAX_DISTANCE = 1024
RMS_EPS = 1e-6

SWA_WIDTH = SWA_Q_HEADS * HEAD_DIM
SWA_KV_WIDTH = SWA_KV_HEADS * HEAD_DIM
MOBA_WIDTH = MOBA_HEADS * HEAD_DIM
SWA_GROUP = SWA_Q_HEADS // SWA_KV_HEADS

T = MOBA_BLOCK
AUG = 128
V_ROWS = HEAD_DIM + 16
NEG = -1e30
N_NEAR = -(-(MAX_DISTANCE + T) // T)
ROW_TILE = 512
VMEM_LIMIT = 56 * 1024 * 1024


def _rel_bucket(dist):
    n = jnp.maximum(dist, 0)
    max_exact = NUM_BUCKETS // 2
    nf = jnp.maximum(n, 1).astype(F32)
    large = max_exact + (jnp.log(nf / max_exact) / math.log(MAX_DISTANCE / max_exact)
                         * (NUM_BUCKETS - max_exact)).astype(jnp.int32)
    large = jnp.minimum(large, NUM_BUCKETS - 1)
    return jnp.where(n < max_exact, n, large)


def _bias_kernel(rel_ref, moba_ref, swa_ref):
    h = pl.program_id(0)
    kk = lax.broadcasted_iota(jnp.int32, (T, T), 0)
    qq = lax.broadcasted_iota(jnp.int32, (T, T), 1)
    rel = qq - kk

    def lookup(bucket, col):
        acc = jnp.zeros((T, T), F32)
        for b in range(NUM_BUCKETS):
            acc = jnp.where(bucket == b, rel_ref[b, col], acc)
        return acc

    for d in range(N_NEAR):
        dist = rel + d * T
        bias = lookup(_rel_bucket(dist), SWA_Q_HEADS + h)
        if d == 0:
            bias = jnp.where(dist >= 0, bias, NEG)
        moba_ref[0, d] = bias
    far = jnp.full((T, T), N_NEAR * T, jnp.int32)
    moba_ref[0, N_NEAR] = lookup(_rel_bucket(far), SWA_Q_HEADS + h)

    for d in range(2):
        dist = rel + d * T
        bias = lookup(_rel_bucket(dist), h)
        valid = (dist >= 0) & (dist < WINDOW)
        swa_ref[0, d] = jnp.where(valid, bias, NEG)


def _bias_tables(rel_bias):
    return pl.pallas_call(
        _bias_kernel,
        grid=(MOBA_HEADS,),
        in_specs=[pl.BlockSpec(memory_space=pltpu.SMEM)],
        out_specs=[
            pl.BlockSpec((1, N_NEAR + 1, T, T), lambda h: (h, 0, 0, 0)),
            pl.BlockSpec((1, 2, T, T), lambda h: (h, 0, 0, 0)),
        ],
        out_shape=[
            jax.ShapeDtypeStruct((MOBA_HEADS, N_NEAR + 1, T, T), F32),
            jax.ShapeDtypeStruct((SWA_Q_HEADS, 2, T, T), F32),
        ],
        name="bias_tables",
    )(rel_bias)


def _inproj_kernel(x_ref, g_ref, wnn_ref, wt_ref,
                   ka_ref, ga_ref, kb_ref, gb_ref,
                   qta_ref, vta_ref, qtb_ref, vtb_ref):
    x = x_ref[0]
    r = lax.rsqrt(jnp.mean(x * x, axis=-1, keepdims=True) + RMS_EPS)
    h = ((x * r) * g_ref[...]).astype(BF16)

    tok = jnp.dot(h, wnn_ref[...], preferred_element_type=F32)
    o = 0
    for ref, width in ((ka_ref, SWA_KV_WIDTH), (ga_ref, SWA_WIDTH),
                       (kb_ref, MOBA_WIDTH), (gb_ref, MOBA_WIDTH)):
        ref[0] = tok[:, o:o + width].astype(BF16)
        o += width

    feat = lax.dot_general(wt_ref[...], h, (((1,), (1,)), ((), ())),
                           preferred_element_type=F32)
    o = 0
    for ref, width in ((qta_ref, SWA_WIDTH), (vta_ref, SWA_KV_WIDTH),
                       (qtb_ref, MOBA_WIDTH), (vtb_ref, MOBA_WIDTH)):
        for c in range(ROW_TILE // T):
            ref[0, c] = feat[o:o + width, c * T:(c + 1) * T].astype(BF16)
        o += width


def _inproj(x, gain, w_nn, w_t):
    B, S, D = x.shape
    nt = S // ROW_TILE
    cpt = ROW_TILE // T
    n_nn = w_nn.shape[1]
    n_t = w_t.shape[0]

    def tok_spec(width):
        return pl.BlockSpec((1, ROW_TILE, width), lambda b, t: (b, t, 0))

    def feat_spec(width):
        return pl.BlockSpec((1, cpt, width, T), lambda b, t: (b, t, 0, 0))

    def tok_shape(width):
        return jax.ShapeDtypeStruct((B, S, width), BF16)

    def feat_shape(width):
        return jax.ShapeDtypeStruct((B, S // T, width, T), BF16)

    return pl.pallas_call(
        _inproj_kernel,
        grid=(B, nt),
        in_specs=[
            pl.BlockSpec((1, ROW_TILE, D), lambda b, t: (b, t, 0)),
            pl.BlockSpec((1, D), lambda b, t: (0, 0)),
            pl.BlockSpec((D, n_nn), lambda b, t: (0, 0)),
            pl.BlockSpec((n_t, D), lambda b, t: (0, 0)),
        ],
        out_specs=[tok_spec(SWA_KV_WIDTH), tok_spec(SWA_WIDTH), tok_spec(MOBA_WIDTH), tok_spec(MOBA_WIDTH),
                   feat_spec(SWA_WIDTH), feat_spec(SWA_KV_WIDTH), feat_spec(MOBA_WIDTH), feat_spec(MOBA_WIDTH)],
        out_shape=[tok_shape(SWA_KV_WIDTH), tok_shape(SWA_WIDTH), tok_shape(MOBA_WIDTH), tok_shape(MOBA_WIDTH),
                   feat_shape(SWA_WIDTH), feat_shape(SWA_KV_WIDTH), feat_shape(MOBA_WIDTH), feat_shape(MOBA_WIDTH)],
        compiler_params=pltpu.CompilerParams(
            dimension_semantics=("arbitrary", "arbitrary"), vmem_limit_bytes=VMEM_LIMIT),
        name="norm_inproj",
    )(x, gain, w_nn, w_t)


def _ones_pad_rows():
    row = lax.broadcasted_iota(jnp.int32, (V_ROWS - HEAD_DIM, T), 0)
    return jnp.where(row == 0, 1.0, 0.0).astype(BF16)


def _tile_update(st, vt_aug, m, acc):
    m_new = jnp.maximum(m, jnp.max(st, axis=0, keepdims=True))
    alpha = jnp.exp(m - m_new)
    p = jnp.exp(st - m_new).astype(BF16)
    acc = acc * alpha + jnp.dot(vt_aug, p, preferred_element_type=F32)
    return m_new, acc


def _swa_kernel(sink_ref, qt_ref, kcur_ref, kprev_ref, vcur_ref, vprev_ref, bias_ref,
                o_ref, ot_ref):
    blk = pl.program_id(1)
    k_cur = kcur_ref[0]
    k_prev = kprev_ref[0]
    pad = _ones_pad_rows()
    zeros_q = jnp.zeros((HEAD_DIM, T), BF16)
    prev_pen = jnp.where(blk == 0, NEG, 0.0).astype(F32)

    for g in range(SWA_KV_HEADS):
        v_cur = jnp.concatenate([vcur_ref[0, 0, g * HEAD_DIM:(g + 1) * HEAD_DIM, :], pad], axis=0)
        v_prev = jnp.concatenate([vprev_ref[0, 0, g * HEAD_DIM:(g + 1) * HEAD_DIM, :], pad], axis=0)
        for hh in range(SWA_GROUP):
            head = g * SWA_GROUP + hh
            qt = qt_ref[0, 0, head * HEAD_DIM:(head + 1) * HEAD_DIM, :]
            parts = [zeros_q] * SWA_KV_HEADS
            parts[g] = qt
            q_pad = jnp.concatenate(parts, axis=0)
            m = jnp.full((1, T), NEG, F32)
            acc = jnp.zeros((V_ROWS, T), F32)
            st = jnp.dot(k_cur, q_pad, preferred_element_type=F32) + bias_ref[head, 0]
            m, acc = _tile_update(st, v_cur, m, acc)
            st = jnp.dot(k_prev, q_pad, preferred_element_type=F32) + (bias_ref[head, 1] + prev_pen)
            m, acc = _tile_update(st, v_prev, m, acc)
            sink = sink_ref[head]
            m_fin = jnp.maximum(m, sink)
            a = jnp.exp(m - m_fin)
            den = acc[HEAD_DIM:HEAD_DIM + 1] * a + jnp.exp(sink - m_fin)
            ot_ref[head * HEAD_DIM:(head + 1) * HEAD_DIM, :] = (acc[:HEAD_DIM] * a) / den
    o_ref[0] = ot_ref[...].T.astype(BF16)


def _swa(sinks, qt_a, k_a, vt_a, swa_bias):
    B, nblk = qt_a.shape[0], qt_a.shape[1]
    S = nblk * T
    prev = lambda b, i: (b, jnp.maximum(i - 1, 0), 0)
    prev4 = lambda b, i: (b, jnp.maximum(i - 1, 0), 0, 0)
    return pl.pallas_call(
        _swa_kernel,
        grid=(B, nblk),
        in_specs=[
            pl.BlockSpec(memory_space=pltpu.SMEM),
            pl.BlockSpec((1, 1, SWA_WIDTH, T), lambda b, i: (b, i, 0, 0)),
            pl.BlockSpec((1, T, SWA_KV_WIDTH), lambda b, i: (b, i, 0)),
            pl.BlockSpec((1, T, SWA_KV_WIDTH), prev),
            pl.BlockSpec((1, 1, SWA_KV_WIDTH, T), lambda b, i: (b, i, 0, 0)),
            pl.BlockSpec((1, 1, SWA_KV_WIDTH, T), prev4),
            pl.BlockSpec((SWA_Q_HEADS, 2, T, T), lambda b, i: (0, 0, 0, 0)),
        ],
        out_specs=pl.BlockSpec((1, T, SWA_WIDTH), lambda b, i: (b, i, 0)),
        out_shape=jax.ShapeDtypeStruct((B, S, SWA_WIDTH), BF16),
        scratch_shapes=[pltpu.VMEM((SWA_WIDTH, T), F32)],
        compiler_params=pltpu.CompilerParams(
            dimension_semantics=("arbitrary", "arbitrary"), vmem_limit_bytes=VMEM_LIMIT),
        name="swa_attention",
    )(sinks, qt_a, k_a, k_a, vt_a, vt_a, swa_bias)


def _moba_kernel(qt_ref, k_ref, vt_ref, bias_ref, o_ref,
                 kaug_ref, vaug_ref, kmean_ref, ot_ref, *, nblk):
    pad = _ones_pad_rows()
    lane = lax.broadcasted_iota(jnp.int32, (T, AUG), 1)
    sel_r = lax.broadcasted_iota(jnp.int32, (2 * HEAD_DIM, AUG), 0)
    sel_c = lax.broadcasted_iota(jnp.int32, (2 * HEAD_DIM, AUG), 1)

    for hl in range(2):
        pick = jnp.where((sel_r == sel_c + hl * HEAD_DIM) & (sel_c < HEAD_DIM), 1.0, 0.0).astype(BF16)

        def prep(r, carry, hl=hl, pick=pick):
            row0 = pl.multiple_of(r * T, T)
            kp = jnp.dot(k_ref[0, pl.ds(row0, T), :], pick, preferred_element_type=F32)
            onehot = jnp.where(lane == HEAD_DIM + r, 1.0, 0.0)
            kaug_ref[hl, pl.ds(row0, T), :] = (kp + onehot).astype(BF16)
            kmean_ref[hl, pl.ds(r, 1), :] = jnp.sum(kp, axis=0, keepdims=True) * (1.0 / T)
            vaug_ref[hl, r] = jnp.concatenate(
                [vt_ref[0, r, hl * HEAD_DIM:(hl + 1) * HEAD_DIM, :], pad], axis=0)
            return carry

        lax.fori_loop(0, nblk, prep, 0)

    blk_id = lax.broadcasted_iota(jnp.int32, (nblk, T), 0)
    zeros_q = jnp.zeros((AUG - HEAD_DIM, T), BF16)
    zeros_tail = jnp.zeros((AUG - HEAD_DIM - nblk, T), BF16)

    def q_block(i, carry):
        for hl in range(2):
            qt = qt_ref[0, i, hl * HEAD_DIM:(hl + 1) * HEAD_DIM, :]
            km = kmean_ref[hl]
            km_hi = km.astype(BF16)
            km_lo = (km - km_hi.astype(F32)).astype(BF16)
            qs = jnp.concatenate([qt, zeros_q], axis=0)
            sc = (jnp.dot(km_hi, qs, preferred_element_type=F32)
                  + jnp.dot(km_lo, qs, preferred_element_type=F32))
            sc = jnp.where(blk_id < i, sc, -jnp.inf)
            chosen = blk_id == i
            for _ in range(MOBA_TOPK):
                mx = jnp.max(sc, axis=0, keepdims=True)
                first = jnp.min(jnp.where(sc == mx, blk_id, nblk), axis=0, keepdims=True)
                hit = (blk_id == first) & (mx > -jnp.inf)
                chosen = chosen | hit
                sc = jnp.where(hit, -jnp.inf, sc)
            sel_bias = jnp.where(chosen, 0.0, NEG).astype(BF16)
            q_aug = jnp.concatenate([qt, sel_bias, zeros_tail], axis=0)

            def kv_block(jj, mc, hl=hl, q_aug=q_aug):
                m, acc = mc
                j = i - jj
                row0 = pl.multiple_of(j * T, T)
                st = jnp.dot(kaug_ref[hl, pl.ds(row0, T), :], q_aug, preferred_element_type=F32)
                st = st + bias_ref[hl, jnp.minimum(jj, N_NEAR)]
                return _tile_update(st, vaug_ref[hl, j], m, acc)

            m0 = jnp.full((1, T), NEG, F32)
            acc0 = jnp.zeros((V_ROWS, T), F32)
            _, acc = lax.fori_loop(0, i + 1, kv_block, (m0, acc0))
            ot_ref[hl * HEAD_DIM:(hl + 1) * HEAD_DIM, :] = acc[:HEAD_DIM] / acc[HEAD_DIM:HEAD_DIM + 1]
        o_ref[0, pl.ds(pl.multiple_of(i * T, T), T), :] = ot_ref[...].T.astype(BF16)
        return carry

    lax.fori_loop(0, nblk, q_block, 0)


def _moba(qt_b, k_b, vt_b, moba_bias):
    B, nblk = qt_b.shape[0], qt_b.shape[1]
    S = nblk * T
    assert HEAD_DIM + nblk <= AUG, "block one-hot must fit the augmented contraction width"
    pair = 2 * HEAD_DIM
    return pl.pallas_call(
        functools.partial(_moba_kernel, nblk=nblk),
        grid=(B, MOBA_HEADS // 2),
        in_specs=[
            pl.BlockSpec((1, nblk, pair, T), lambda b, h: (b, 0, h, 0)),
            pl.BlockSpec((1, S, pair), lambda b, h: (b, 0, h)),
            pl.BlockSpec((1, nblk, pair, T), lambda b, h: (b, 0, h, 0)),
            pl.BlockSpec((2, N_NEAR + 1, T, T), lambda b, h: (h, 0, 0, 0)),
        ],
        out_specs=pl.BlockSpec((1, S, pair), lambda b, h: (b, 0, h)),
        out_shape=jax.ShapeDtypeStruct((B, S, MOBA_WIDTH), BF16),
        scratch_shapes=[
            pltpu.VMEM((2, S, AUG), BF16),
            pltpu.VMEM((2, nblk, V_ROWS, T), BF16),
            pltpu.VMEM((2, nblk, AUG), F32),
            pltpu.VMEM((pair, T), F32),
        ],
        compiler_params=pltpu.CompilerParams(
            dimension_semantics=("arbitrary", "arbitrary"), vmem_limit_bytes=VMEM_LIMIT),
        name="moba_attention",
    )(qt_b, k_b, vt_b, moba_bias)


def _sigmoid(v):
    return 1.0 / (1.0 + jnp.exp(-v))


def _rms(v, gain):
    r = lax.rsqrt(jnp.mean(v * v, axis=-1, keepdims=True) + RMS_EPS)
    return (v * r) * gain


def _outproj_kernel(x_ref, ao_ref, bo_ref, ga_ref, gb_ref, p_ref,
                    wo_ref, pg_ref, wg_ref, wp_ref, fg_ref, o_ref, *, final):
    ga = ga_ref[0].astype(F32)
    gb = gb_ref[0].astype(F32)
    ma = (ao_ref[0].astype(F32) * (ga * _sigmoid(ga))).astype(BF16)
    mb = (bo_ref[0].astype(F32) * (gb * _sigmoid(gb))).astype(BF16)
    x1 = (x_ref[0]
          + jnp.dot(ma, wo_ref[0:SWA_WIDTH, :], preferred_element_type=F32)
          + jnp.dot(mb, wo_ref[SWA_WIDTH:, :], preferred_element_type=F32))
    n = _rms(x1, pg_ref[...]).astype(BF16)
    gate = _sigmoid(jnp.dot(n, wg_ref[...], preferred_element_type=F32))
    up = jnp.dot(p_ref[0].astype(BF16), wp_ref[...], preferred_element_type=F32)
    y = x1 + up * gate
    if final:
        y = _rms(y, fg_ref[...])
    o_ref[0] = y


def _outproj(x, a_out, b_out, g_a, g_b, p, w_out, ple_gain, w_gate, w_proj, final_gain, final):
    B, S, D = x.shape
    nt = S // ROW_TILE
    ple = p.shape[-1]

    def tok(width):
        return pl.BlockSpec((1, ROW_TILE, width), lambda b, t: (b, t, 0))

    def whole(shape):
        return pl.BlockSpec(shape, lambda b, t: (0,) * len(shape))

    return pl.pallas_call(
        functools.partial(_outproj_kernel, final=final),
        grid=(B, nt),
        in_specs=[tok(D), tok(SWA_WIDTH), tok(MOBA_WIDTH), tok(SWA_WIDTH), tok(MOBA_WIDTH), tok(ple),
                  whole((D, D)), whole((1, D)), whole((D, D)), whole((ple, D)), whole((1, D))],
        out_specs=tok(D),
        out_shape=jax.ShapeDtypeStruct((B, S, D), F32),
        compiler_params=pltpu.CompilerParams(
            dimension_semantics=("arbitrary", "arbitrary"), vmem_limit_bytes=VMEM_LIMIT),
        name="outproj_ple",
    )(x, a_out, b_out, g_a, g_b, p, w_out, ple_gain, w_gate, w_proj, final_gain)


def _split_in_weights(w):
    widths = (SWA_WIDTH, SWA_KV_WIDTH, SWA_KV_WIDTH, SWA_WIDTH,
              MOBA_WIDTH, MOBA_WIDTH, MOBA_WIDTH, MOBA_WIDTH)
    offs = [0]
    for wd in widths:
        offs.append(offs[-1] + wd)
    a_q, a_k, a_v, a_g, b_q, b_k, b_v, b_g = (w[:, offs[n]:offs[n + 1]] for n in range(8))
    scale = HEAD_DIM ** -0.5
    w_nn = jnp.concatenate([a_k, a_g, b_k, b_g], axis=1).astype(BF16)
    w_t = jnp.concatenate([a_q * scale, a_v, b_q * scale, b_v], axis=1).T.astype(BF16)
    return w_nn, w_t


def kernel(x, p, norm_in, w_in, sinks, rel_bias, w_out, ple_norm, w_ple_gate, w_ple_proj, final_norm):
    B, S, D = x.shape
    depth = p.shape[0]
    assert S % ROW_TILE == 0 and ROW_TILE % T == 0
    moba_bias, swa_bias = _bias_tables(rel_bias)
    for i in range(depth):
        w_nn, w_t = _split_in_weights(w_in[i])
        k_a, g_a, k_b, g_b, qt_a, vt_a, qt_b, vt_b = _inproj(x, norm_in[i][None, :], w_nn, w_t)
        a_out = _swa(sinks[i], qt_a, k_a, vt_a, swa_bias)
        b_out = _moba(qt_b, k_b, vt_b, moba_bias)
        x = _outproj(x, a_out, b_out, g_a, g_b, p[i],
                     w_out[i].astype(BF16), ple_norm[i][None, :], w_ple_gate[i].astype(BF16),
                     w_ple_proj[i].astype(BF16), final_norm[None, :], final=(i == depth - 1))
    return x
```

```python
import functools
import math

import jax
import jax.numpy as jnp
from jax import lax
from jax.experimental import pallas as pl
from jax.experimental.pallas import tpu as pltpu

F32 = jnp.float32
BF16 = jnp.bfloat16

HEAD_DIM = 64
SWA_Q_HEADS = 8
SWA_KV_HEADS = 2
MOBA_HEADS = 8
WINDOW = 128
MOBA_BLOCK = 256
MOBA_TOPK = 3
NUM_BUCKETS = 32
MAX_DISTANCE = 1024
RMS_EPS = 1e-6

SWA_WIDTH = SWA_Q_HEADS * HEAD_DIM
SWA_KV_WIDTH = SWA_KV_HEADS * HEAD_DIM
MOBA_WIDTH = MOBA_HEADS * HEAD_DIM
SWA_GROUP = SWA_Q_HEADS // SWA_KV_HEADS

T = MOBA_BLOCK
AUG = 128
V_ROWS = HEAD_DIM + 16
NEG = -1e30
N_NEAR = -(-(MAX_DISTANCE + T) // T)
ROW_TILE = 512
VMEM_LIMIT = 56 * 1024 * 1024


def _rel_bucket(dist):
    n = jnp.maximum(dist, 0)
    max_exact = NUM_BUCKETS // 2
    nf = jnp.maximum(n, 1).astype(F32)
    large = max_exact + (jnp.log(nf / max_exact) / math.log(MAX_DISTANCE / max_exact)
                         * (NUM_BUCKETS - max_exact)).astype(jnp.int32)
    large = jnp.minimum(large, NUM_BUCKETS - 1)
    return jnp.where(n < max_exact, n, large)


def _bias_kernel(rel_ref, moba_ref, swa_ref):
    h = pl.program_id(0)
    kk = lax.broadcasted_iota(jnp.int32, (T, T), 0)
    qq = lax.broadcasted_iota(jnp.int32, (T, T), 1)
    rel = qq - kk

    def lookup(bucket, col):
        acc = jnp.zeros((T, T), F32)
        for b in range(NUM_BUCKETS):
            acc = jnp.where(bucket == b, rel_ref[b, col], acc)
        return acc

    for d in range(N_NEAR):
        dist = rel + d * T
        bias = lookup(_rel_bucket(dist), SWA_Q_HEADS + h)
        if d == 0:
            bias = jnp.where(dist >= 0, bias, NEG)
        moba_ref[0, d] = bias
    far = jnp.full((T, T), N_NEAR * T, jnp.int32)
    moba_ref[0, N_NEAR] = lookup(_rel_bucket(far), SWA_Q_HEADS + h)

    for d in range(2):
        dist = rel + d * T
        bias = lookup(_rel_bucket(dist), h)
        valid = (dist >= 0) & (dist < WINDOW)
        swa_ref[0, d] = jnp.where(valid, bias, NEG)


def _bias_tables(rel_bias):
    return pl.pallas_call(
        _bias_kernel,
        grid=(MOBA_HEADS,),
        in_specs=[pl.BlockSpec(memory_space=pltpu.SMEM)],
        out_specs=[
            pl.BlockSpec((1, N_NEAR + 1, T, T), lambda h: (h, 0, 0, 0)),
            pl.BlockSpec((1, 2, T, T), lambda h: (h, 0, 0, 0)),
        ],
        out_shape=[
            jax.ShapeDtypeStruct((MOBA_HEADS, N_NEAR + 1, T, T), F32),
            jax.ShapeDtypeStruct((SWA_Q_HEADS, 2, T, T), F32),
        ],
        name="bias_tables",
    )(rel_bias)


def _inproj_kernel(x_ref, g_ref, wnn_ref, wt_ref,
                   ka_ref, ga_ref, kb_ref, gb_ref,
                   qta_ref, vta_ref, qtb_ref, vtb_ref):
    x = x_ref[0]
    r = lax.rsqrt(jnp.mean(x * x, axis=-1, keepdims=True) + RMS_EPS)
    h = ((x * r) * g_ref[...]).astype(BF16)

    tok = jnp.dot(h, wnn_ref[...], preferred_element_type=F32)
    o = 0
    for ref, width in ((ka_ref, SWA_KV_WIDTH), (ga_ref, SWA_WIDTH),
                       (kb_ref, MOBA_WIDTH), (gb_ref, MOBA_WIDTH)):
        ref[0] = tok[:, o:o + width].astype(BF16)
        o += width

    feat = lax.dot_general(wt_ref[...], h, (((1,), (1,)), ((), ())),
                           preferred_element_type=F32)
    o = 0
    for ref, width in ((qta_ref, SWA_WIDTH), (vta_ref, SWA_KV_WIDTH),
                       (qtb_ref, MOBA_WIDTH), (vtb_ref, MOBA_WIDTH)):
        for c in range(ROW_TILE // T):
            ref[0, c] = feat[o:o + width, c * T:(c + 1) * T].astype(BF16)
        o += width


def _inproj(x, gain, w_nn, w_t):
    B, S, D = x.shape
    nt = S // ROW_TILE
    cpt = ROW_TILE // T
    n_nn = w_nn.shape[1]
    n_t = w_t.shape[0]

    def tok_spec(width):
        return pl.BlockSpec((1, ROW_TILE, width), lambda b, t: (b, t, 0))

    def feat_spec(width):
        return pl.BlockSpec((1, cpt, width, T), lambda b, t: (b, t, 0, 0))

    def tok_shape(width):
        return jax.ShapeDtypeStruct((B, S, width), BF16)

    def feat_shape(width):
        return jax.ShapeDtypeStruct((B, S // T, width, T), BF16)

    return pl.pallas_call(
        _inproj_kernel,
        grid=(B, nt),
        in_specs=[
            pl.BlockSpec((1, ROW_TILE, D), lambda b, t: (b, t, 0)),
            pl.BlockSpec((1, D), lambda b, t: (0, 0)),
            pl.BlockSpec((D, n_nn), lambda b, t: (0, 0)),
            pl.BlockSpec((n_t, D), lambda b, t: (0, 0)),
        ],
        out_specs=[tok_spec(SWA_KV_WIDTH), tok_spec(SWA_WIDTH), tok_spec(MOBA_WIDTH), tok_spec(MOBA_WIDTH),
                   feat_spec(SWA_WIDTH), feat_spec(SWA_KV_WIDTH), feat_spec(MOBA_WIDTH), feat_spec(MOBA_WIDTH)],
        out_shape=[tok_shape(SWA_KV_WIDTH), tok_shape(SWA_WIDTH), tok_shape(MOBA_WIDTH), tok_shape(MOBA_WIDTH),
                   feat_shape(SWA_WIDTH), feat_shape(SWA_KV_WIDTH), feat_shape(MOBA_WIDTH), feat_shape(MOBA_WIDTH)],
        compiler_params=pltpu.CompilerParams(
            dimension_semantics=("arbitrary", "arbitrary"), vmem_limit_bytes=VMEM_LIMIT),
        name="norm_inproj",
    )(x, gain, w_nn, w_t)


def _ones_pad_rows():
    row = lax.broadcasted_iota(jnp.int32, (V_ROWS - HEAD_DIM, T), 0)
    return jnp.where(row == 0, 1.0, 0.0).astype(BF16)


def _tile_update(st, vt_aug, m, acc):
    m_new = jnp.maximum(m, jnp.max(st, axis=0, keepdims=True))
    alpha = jnp.exp(m - m_new)
    p = jnp.exp(st - m_new).astype(BF16)
    acc = acc * alpha + jnp.dot(vt_aug, p, preferred_element_type=F32)
    return m_new, acc


def _swa_kernel(sink_ref, qt_ref, kcur_ref, kprev_ref, vcur_ref, vprev_ref, bias_ref,
                o_ref, ot_ref):
    blk = pl.program_id(1)
    k_cur = kcur_ref[0]
    k_prev = kprev_ref[0]
    pad = _ones_pad_rows()
    zeros_q = jnp.zeros((HEAD_DIM, T), BF16)
    prev_pen = jnp.where(blk == 0, NEG, 0.0).astype(F32)

    for g in range(SWA_KV_HEADS):
        v_cur = jnp.concatenate([vcur_ref[0, 0, g * HEAD_DIM:(g + 1) * HEAD_DIM, :], pad], axis=0)
        v_prev = jnp.concatenate([vprev_ref[0, 0, g * HEAD_DIM:(g + 1) * HEAD_DIM, :], pad], axis=0)
        for hh in range(SWA_GROUP):
            head = g * SWA_GROUP + hh
            qt = qt_ref[0, 0, head * HEAD_DIM:(head + 1) * HEAD_DIM, :]
            parts = [zeros_q] * SWA_KV_HEADS
            parts[g] = qt
            q_pad = jnp.concatenate(parts, axis=0)
            m = jnp.full((1, T), NEG, F32)
            acc = jnp.zeros((V_ROWS, T), F32)
            st = jnp.dot(k_cur, q_pad, preferred_element_type=F32) + bias_ref[head, 0]
            m, acc = _tile_update(st, v_cur, m, acc)
            st = jnp.dot(k_prev, q_pad, preferred_element_type=F32) + (bias_ref[head, 1] + prev_pen)
            m, acc = _tile_update(st, v_prev, m, acc)
            sink = sink_ref[head]
            m_fin = jnp.maximum(m, sink)
            a = jnp.exp(m - m_fin)
            den = acc[HEAD_DIM:HEAD_DIM + 1] * a + jnp.exp(sink - m_fin)
            ot_ref[head * HEAD_DIM:(head + 1) * HEAD_DIM, :] = (acc[:HEAD_DIM] * a) / den
    o_ref[0] = ot_ref[...].T.astype(BF16)


def _swa(sinks, qt_a, k_a, vt_a, swa_bias):
    B, nblk = qt_a.shape[0], qt_a.shape[1]
    S = nblk * T
    prev = lambda b, i: (b, jnp.maximum(i - 1, 0), 0)
    prev4 = lambda b, i: (b, jnp.maximum(i - 1, 0), 0, 0)
    return pl.pallas_call(
        _swa_kernel,
        grid=(B, nblk),
        in_specs=[
            pl.BlockSpec(memory_space=pltpu.SMEM),
            pl.BlockSpec((1, 1, SWA_WIDTH, T), lambda b, i: (b, i, 0, 0)),
            pl.BlockSpec((1, T, SWA_KV_WIDTH), lambda b, i: (b, i, 0)),
            pl.BlockSpec((1, T, SWA_KV_WIDTH), prev),
            pl.BlockSpec((1, 1, SWA_KV_WIDTH, T), lambda b, i: (b, i, 0, 0)),
            pl.BlockSpec((1, 1, SWA_KV_WIDTH, T), prev4),
            pl.BlockSpec((SWA_Q_HEADS, 2, T, T), lambda b, i: (0, 0, 0, 0)),
        ],
        out_specs=pl.BlockSpec((1, T, SWA_WIDTH), lambda b, i: (b, i, 0)),
        out_shape=jax.ShapeDtypeStruct((B, S, SWA_WIDTH), BF16),
        scratch_shapes=[pltpu.VMEM((SWA_WIDTH, T), F32)],
        compiler_params=pltpu.CompilerParams(
            dimension_semantics=("arbitrary", "arbitrary"), vmem_limit_bytes=VMEM_LIMIT),
        name="swa_attention",
    )(sinks, qt_a, k_a, k_a, vt_a, vt_a, swa_bias)


def _moba_kernel(qt_ref, k_ref, vt_ref, bias_ref, o_ref,
                 kaug_ref, vaug_ref, kmean_ref, ot_ref, s00_ref, s01_ref, s10_ref, s11_ref, *, nblk):
    s_refs = ((s00_ref, s01_ref), (s10_ref, s11_ref))
    pad = _ones_pad_rows()
    lane = lax.broadcasted_iota(jnp.int32, (T, AUG), 1)
    sel_r = lax.broadcasted_iota(jnp.int32, (2 * HEAD_DIM, AUG), 0)
    sel_c = lax.broadcasted_iota(jnp.int32, (2 * HEAD_DIM, AUG), 1)

    for hl in range(2):
        pick = jnp.where((sel_r == sel_c + hl * HEAD_DIM) & (sel_c < HEAD_DIM), 1.0, 0.0).astype(BF16)

        def prep(r, carry, hl=hl, pick=pick):
            row0 = pl.multiple_of(r * T, T)
            kp = jnp.dot(k_ref[0, pl.ds(row0, T), :], pick, preferred_element_type=F32)
            onehot = jnp.where(lane == HEAD_DIM + r, 1.0, 0.0)
            kaug_ref[hl, pl.ds(row0, T), :] = (kp + onehot).astype(BF16)
            kmean_ref[hl, pl.ds(r, 1), :] = jnp.sum(kp, axis=0, keepdims=True) * (1.0 / T)
            vaug_ref[hl, r] = jnp.concatenate(
                [vt_ref[0, r, hl * HEAD_DIM:(hl + 1) * HEAD_DIM, :], pad], axis=0)
            return carry

        lax.fori_loop(0, nblk, prep, 0)

    blk_id = lax.broadcasted_iota(jnp.int32, (nblk, T), 0)
    zeros_q = jnp.zeros((AUG - HEAD_DIM, T), BF16)
    zeros_tail = jnp.zeros((AUG - HEAD_DIM - nblk, T), BF16)

    def q_block(i, carry):
        q_augs = []
        for hl in range(2):
            qt = qt_ref[0, i, hl * HEAD_DIM:(hl + 1) * HEAD_DIM, :]
            km = kmean_ref[hl]
            km_hi = km.astype(BF16)
            km_lo = (km - km_hi.astype(F32)).astype(BF16)
            qs = jnp.concatenate([qt, zeros_q], axis=0)
            sc = (jnp.dot(km_hi, qs, preferred_element_type=F32)
                  + jnp.dot(km_lo, qs, preferred_element_type=F32))
            sc = jnp.where(blk_id < i, sc, -jnp.inf)
            chosen = blk_id == i
            for _ in range(MOBA_TOPK):
                mx = jnp.max(sc, axis=0, keepdims=True)
                first = jnp.min(jnp.where(sc == mx, blk_id, nblk), axis=0, keepdims=True)
                hit = (blk_id == first) & (mx > -jnp.inf)
                chosen = chosen | hit
                sc = jnp.where(hit, -jnp.inf, sc)
            sel_bias = jnp.where(chosen, 0.0, NEG).astype(BF16)
            q_augs.append(jnp.concatenate([qt, sel_bias, zeros_tail], axis=0))

        npairs = (i + 2) // 2

        def score(hl, slot, j, m):
            row0 = pl.multiple_of(j * T, T)
            st = jnp.dot(kaug_ref[hl, pl.ds(row0, T), :], q_augs[hl], preferred_element_type=F32)
            st = st + bias_ref[hl, jnp.clip(i - j, 0, N_NEAR)]
            s_refs[hl][slot][...] = st
            m_new = jnp.maximum(m, jnp.max(st, axis=0, keepdims=True))
            return m_new, jnp.exp(m - m_new)

        def accumulate(hl, slot, j, m, alpha, acc):
            p = jnp.exp(s_refs[hl][slot][...] - m).astype(BF16)
            return acc * alpha + jnp.dot(vaug_ref[hl, j], p, preferred_element_type=F32)

        def half_step(state, slot, j, last=False):
            if last:
                return [accumulate(hl, slot, j, *state[hl]) for hl in range(2)]
            nxt = [score(hl, 1 - slot, j + 1, state[hl][0]) for hl in range(2)]
            return tuple(nxt[hl] + (accumulate(hl, slot, j, *state[hl]),) for hl in range(2))

        m0 = jnp.full((1, T), NEG, F32)
        state = tuple(score(hl, 0, 0, m0) + (jnp.zeros((V_ROWS, T), F32),) for hl in range(2))

        def kv_pair(u, state):
            state = half_step(state, 0, 2 * u)
            return half_step(state, 1, 2 * u + 1)

        state = lax.fori_loop(0, npairs - 1, kv_pair, state)
        j_last = 2 * (npairs - 1)
        accs = half_step(half_step(state, 0, j_last), 1, j_last + 1, last=True)
        for hl in range(2):
            acc = accs[hl]
            ot_ref[hl * HEAD_DIM:(hl + 1) * HEAD_DIM, :] = acc[:HEAD_DIM] / acc[HEAD_DIM:HEAD_DIM + 1]
        o_ref[0, pl.ds(pl.multiple_of(i * T, T), T), :] = ot_ref[...].T.astype(BF16)
        return carry

    lax.fori_loop(0, nblk, q_block, 0)


def _moba(qt_b, k_b, vt_b, moba_bias):
    B, nblk = qt_b.shape[0], qt_b.shape[1]
    S = nblk * T
    assert HEAD_DIM + nblk <= AUG, "block one-hot must fit the augmented contraction width"
    pair = 2 * HEAD_DIM
    return pl.pallas_call(
        functools.partial(_moba_kernel, nblk=nblk),
        grid=(B, MOBA_HEADS // 2),
        in_specs=[
            pl.BlockSpec((1, nblk, pair, T), lambda b, h: (b, 0, h, 0)),
            pl.BlockSpec((1, S, pair), lambda b, h: (b, 0, h)),
            pl.BlockSpec((1, nblk, pair, T), lambda b, h: (b, 0, h, 0)),
            pl.BlockSpec((2, N_NEAR + 1, T, T), lambda b, h: (h, 0, 0, 0)),
        ],
        out_specs=pl.BlockSpec((1, S, pair), lambda b, h: (b, 0, h)),
        out_shape=jax.ShapeDtypeStruct((B, S, MOBA_WIDTH), BF16),
        scratch_shapes=[
            pltpu.VMEM((2, S, AUG), BF16),
            pltpu.VMEM((2, nblk, V_ROWS, T), BF16),
            pltpu.VMEM((2, nblk, AUG), F32),
            pltpu.VMEM((pair, T), F32),
        ] + [pltpu.VMEM((T, T), F32)] * 4,
        compiler_params=pltpu.CompilerParams(
            dimension_semantics=("arbitrary", "arbitrary"), vmem_limit_bytes=VMEM_LIMIT),
        name="moba_attention",
    )(qt_b, k_b, vt_b, moba_bias)


def _sigmoid(v):
    return 1.0 / (1.0 + jnp.exp(-v))


def _rms(v, gain):
    r = lax.rsqrt(jnp.mean(v * v, axis=-1, keepdims=True) + RMS_EPS)
    return (v * r) * gain


def _outproj_kernel(x_ref, ao_ref, bo_ref, ga_ref, gb_ref, p_ref,
                    wo_ref, pg_ref, wg_ref, wp_ref, fg_ref, o_ref, *, final):
    ga = ga_ref[0].astype(F32)
    gb = gb_ref[0].astype(F32)
    ma = (ao_ref[0].astype(F32) * (ga * _sigmoid(ga))).astype(BF16)
    mb = (bo_ref[0].astype(F32) * (gb * _sigmoid(gb))).astype(BF16)
    x1 = (x_ref[0]
          + jnp.dot(ma, wo_ref[0:SWA_WIDTH, :], preferred_element_type=F32)
          + jnp.dot(mb, wo_ref[SWA_WIDTH:, :], preferred_element_type=F32))
    n = _rms(x1, pg_ref[...]).astype(BF16)
    gate = _sigmoid(jnp.dot(n, wg_ref[...], preferred_element_type=F32))
    up = jnp.dot(p_ref[0].astype(BF16), wp_ref[...], preferred_element_type=F32)
    y = x1 + up * gate
    if final:
        y = _rms(y, fg_ref[...])
    o_ref[0] = y


def _outproj(x, a_out, b_out, g_a, g_b, p, w_out, ple_gain, w_gate, w_proj, final_gain, final):
    B, S, D = x.shape
    nt = S // ROW_TILE
    ple = p.shape[-1]

    def tok(width):
        return pl.BlockSpec((1, ROW_TILE, width), lambda b, t: (b, t, 0))

    def whole(shape):
        return pl.BlockSpec(shape, lambda b, t: (0,) * len(shape))

    return pl.pallas_call(
        functools.partial(_outproj_kernel, final=final),
        grid=(B, nt),
        in_specs=[tok(D), tok(SWA_WIDTH), tok(MOBA_WIDTH), tok(SWA_WIDTH), tok(MOBA_WIDTH), tok(ple),
                  whole((D, D)), whole((1, D)), whole((D, D)), whole((ple, D)), whole((1, D))],
        out_specs=tok(D),
        out_shape=jax.ShapeDtypeStruct((B, S, D), F32),
        compiler_params=pltpu.CompilerParams(
            dimension_semantics=("arbitrary", "arbitrary"), vmem_limit_bytes=VMEM_LIMIT),
        name="outproj_ple",
    )(x, a_out, b_out, g_a, g_b, p, w_out, ple_gain, w_gate, w_proj, final_gain)


def _split_in_weights(w):
    widths = (SWA_WIDTH, SWA_KV_WIDTH, SWA_KV_WIDTH, SWA_WIDTH,
              MOBA_WIDTH, MOBA_WIDTH, MOBA_WIDTH, MOBA_WIDTH)
    offs = [0]
    for wd in widths:
        offs.append(offs[-1] + wd)
    a_q, a_k, a_v, a_g, b_q, b_k, b_v, b_g = (w[:, offs[n]:offs[n + 1]] for n in range(8))
    scale = HEAD_DIM ** -0.5
    w_nn = jnp.concatenate([a_k, a_g, b_k, b_g], axis=1).astype(BF16)
    w_t = jnp.concatenate([a_q * scale, a_v, b_q * scale, b_v], axis=1).T.astype(BF16)
    return w_nn, w_t


def kernel(x, p, norm_in, w_in, sinks, rel_bias, w_out, ple_norm, w_ple_gate, w_ple_proj, final_norm):
    B, S, D = x.shape
    depth = p.shape[0]
    assert S % ROW_TILE == 0 and ROW_TILE % T == 0
    moba_bias, swa_bias = _bias_tables(rel_bias)
    for i in range(depth):
        w_nn, w_t = _split_in_weights(w_in[i])
        k_a, g_a, k_b, g_b, qt_a, vt_a, qt_b, vt_b = _inproj(x, norm_in[i][None, :], w_nn, w_t)
        a_out = _swa(sinks[i], qt_a, k_a, vt_a, swa_bias)
        b_out = _moba(qt_b, k_b, vt_b, moba_bias)
        x = _outproj(x, a_out, b_out, g_a, g_b, p[i],
                     w_out[i].astype(BF16), ple_norm[i][None, :], w_ple_gate[i].astype(BF16),
                     w_ple_proj[i].astype(BF16), final_norm[None, :], final=(i == depth - 1))
    return x
```

```python
import functools
import math

import jax
import jax.numpy as jnp
from jax import lax
from jax.experimental import pallas as pl
from jax.experimental.pallas import tpu as pltpu

F32 = jnp.float32
BF16 = jnp.bfloat16

HEAD_DIM = 64
SWA_Q_HEADS = 8
SWA_KV_HEADS = 2
MOBA_HEADS = 8
WINDOW = 128
MOBA_BLOCK = 256
MOBA_TOPK = 3
NUM_BUCKETS = 32
MAX_DISTANCE = 1024
RMS_EPS = 1e-6

SWA_WIDTH = SWA_Q_HEADS * HEAD_DIM
SWA_KV_WIDTH = SWA_KV_HEADS * HEAD_DIM
MOBA_WIDTH = MOBA_HEADS * HEAD_DIM
SWA_GROUP = SWA_Q_HEADS // SWA_KV_HEADS

T = MOBA_BLOCK
AUG = 128
V_ROWS = HEAD_DIM + 16
NEG = -1e30
LOG2E = math.log2(math.e)
N_NEAR = -(-(MAX_DISTANCE + T) // T)
KEY_GROUP = 8
QUERY_GROUP = 4
assert WINDOW <= T // 2
ROW_TILE = 512
VMEM_LIMIT = 56 * 1024 * 1024


def _rel_bucket(dist):
    n = jnp.maximum(dist, 0)
    max_exact = NUM_BUCKETS // 2
    nf = jnp.maximum(n, 1).astype(F32)
    large = max_exact + (jnp.log(nf / max_exact) / math.log(MAX_DISTANCE / max_exact)
                         * (NUM_BUCKETS - max_exact)).astype(jnp.int32)
    large = jnp.minimum(large, NUM_BUCKETS - 1)
    return jnp.where(n < max_exact, n, large)


def _bias_kernel(rel_ref, moba_ref, swa_ref):
    h = pl.program_id(0)
    kk = lax.broadcasted_iota(jnp.int32, (T, T), 0)
    qq = lax.broadcasted_iota(jnp.int32, (T, T), 1)
    rel = qq - kk

    def lookup(bucket, col):
        acc = jnp.zeros((T, T), F32)
        for b in range(NUM_BUCKETS):
            acc = jnp.where(bucket == b, rel_ref[b, col] * LOG2E, acc)
        return acc

    for d in range(N_NEAR):
        dist = rel + d * T
        bias = lookup(_rel_bucket(dist), SWA_Q_HEADS + h)
        if d == 0:
            bias = jnp.where(dist >= 0, bias, NEG)
        moba_ref[0, d] = bias
    far = jnp.full((T, T), N_NEAR * T, jnp.int32)
    moba_ref[0, N_NEAR] = lookup(_rel_bucket(far), SWA_Q_HEADS + h)

    for d in range(2):
        dist = rel + d * T
        bias = lookup(_rel_bucket(dist), h)
        valid = (dist >= 0) & (dist < WINDOW)
        swa_ref[0, d] = jnp.where(valid, bias, NEG)


def _bias_tables(rel_bias):
    return pl.pallas_call(
        _bias_kernel,
        grid=(MOBA_HEADS,),
        in_specs=[pl.BlockSpec(memory_space=pltpu.SMEM)],
        out_specs=[
            pl.BlockSpec((1, N_NEAR + 1, T, T), lambda h: (h, 0, 0, 0)),
            pl.BlockSpec((1, 2, T, T), lambda h: (h, 0, 0, 0)),
        ],
        out_shape=[
            jax.ShapeDtypeStruct((MOBA_HEADS, N_NEAR + 1, T, T), F32),
            jax.ShapeDtypeStruct((SWA_Q_HEADS, 2, T, T), F32),
        ],
        name="bias_tables",
    )(rel_bias)


def _inproj_kernel(x_ref, g_ref, wnn_ref, wt_ref,
                   ka_ref, ga_ref, kb_ref, gb_ref,
                   qta_ref, vta_ref, qtb_ref, vtb_ref):
    x = x_ref[0]
    r = lax.rsqrt(jnp.mean(x * x, axis=-1, keepdims=True) + RMS_EPS)
    h = ((x * r) * g_ref[...]).astype(BF16)

    tok = jnp.dot(h, wnn_ref[...], preferred_element_type=F32)
    o = 0
    for ref, width in ((ka_ref, SWA_KV_WIDTH), (ga_ref, SWA_WIDTH),
                       (kb_ref, MOBA_WIDTH), (gb_ref, MOBA_WIDTH)):
        ref[0] = tok[:, o:o + width].astype(BF16)
        o += width

    feat = lax.dot_general(wt_ref[...], h, (((1,), (1,)), ((), ())),
                           preferred_element_type=F32)
    o = 0
    for ref, width in ((qta_ref, SWA_WIDTH), (vta_ref, SWA_KV_WIDTH),
                       (qtb_ref, MOBA_WIDTH), (vtb_ref, MOBA_WIDTH)):
        for c in range(ROW_TILE // T):
            ref[0, c] = feat[o:o + width, c * T:(c + 1) * T].astype(BF16)
        o += width


def _inproj(x, gain, w_nn, w_t):
    B, S, D = x.shape
    nt = S // ROW_TILE
    cpt = ROW_TILE // T
    n_nn = w_nn.shape[1]
    n_t = w_t.shape[0]

    def tok_spec(width):
        return pl.BlockSpec((1, ROW_TILE, width), lambda b, t: (b, t, 0))

    def feat_spec(width):
        return pl.BlockSpec((1, cpt, width, T), lambda b, t: (b, t, 0, 0))

    def tok_shape(width):
        return jax.ShapeDtypeStruct((B, S, width), BF16)

    def feat_shape(width):
        return jax.ShapeDtypeStruct((B, S // T, width, T), BF16)

    return pl.pallas_call(
        _inproj_kernel,
        grid=(B, nt),
        in_specs=[
            pl.BlockSpec((1, ROW_TILE, D), lambda b, t: (b, t, 0)),
            pl.BlockSpec((1, D), lambda b, t: (0, 0)),
            pl.BlockSpec((D, n_nn), lambda b, t: (0, 0)),
            pl.BlockSpec((n_t, D), lambda b, t: (0, 0)),
        ],
        out_specs=[tok_spec(SWA_KV_WIDTH), tok_spec(SWA_WIDTH), tok_spec(MOBA_WIDTH), tok_spec(MOBA_WIDTH),
                   feat_spec(SWA_WIDTH), feat_spec(SWA_KV_WIDTH), feat_spec(MOBA_WIDTH), feat_spec(MOBA_WIDTH)],
        out_shape=[tok_shape(SWA_KV_WIDTH), tok_shape(SWA_WIDTH), tok_shape(MOBA_WIDTH), tok_shape(MOBA_WIDTH),
                   feat_shape(SWA_WIDTH), feat_shape(SWA_KV_WIDTH), feat_shape(MOBA_WIDTH), feat_shape(MOBA_WIDTH)],
        compiler_params=pltpu.CompilerParams(
            dimension_semantics=("arbitrary", "arbitrary"), vmem_limit_bytes=VMEM_LIMIT),
        name="norm_inproj",
    )(x, gain, w_nn, w_t)


def _ones_pad_rows(width):
    row = lax.broadcasted_iota(jnp.int32, (V_ROWS - HEAD_DIM, width), 0)
    return jnp.where(row == 0, 1.0, 0.0).astype(BF16)


def _swa_kernel(sink_ref, qt_ref, kcur_ref, kprev_ref, vcur_ref, vprev_ref, bias_ref,
                o_ref, ot_ref, sc_ref, sp_ref):
    blk = pl.program_id(1)
    half = T // 2
    k_cur = kcur_ref[0]
    k_prev = kprev_ref[0, half:, :]
    zeros_q = jnp.zeros((HEAD_DIM, T), BF16)
    prev_pen = jnp.where(blk == 0, NEG, 0.0).astype(F32)

    maxes = []
    for head in range(SWA_Q_HEADS):
        g = head // SWA_GROUP
        qt = qt_ref[0, 0, head * HEAD_DIM:(head + 1) * HEAD_DIM, :]
        parts = [zeros_q] * SWA_KV_HEADS
        parts[g] = qt
        q_pad = jnp.concatenate(parts, axis=0)
        s_cur = jnp.dot(k_cur, q_pad, preferred_element_type=F32) + bias_ref[head, 0]
        s_prev = jnp.dot(k_prev, q_pad, preferred_element_type=F32) + (bias_ref[head, 1, half:, :] + prev_pen)
        sc_ref[head] = s_cur
        sp_ref[head] = s_prev
        m = jnp.maximum(jnp.max(s_cur, axis=0, keepdims=True), jnp.max(s_prev, axis=0, keepdims=True))
        maxes.append(jnp.maximum(m, sink_ref[head] * LOG2E))

    for head in range(SWA_Q_HEADS):
        g = head // SWA_GROUP
        rows = slice(g * HEAD_DIM, (g + 1) * HEAD_DIM)
        v_cur = jnp.concatenate([vcur_ref[0, 0, rows, :], _ones_pad_rows(T)], axis=0)
        v_prev = jnp.concatenate([vprev_ref[0, 0, rows, half:], _ones_pad_rows(half)], axis=0)
        m = maxes[head]
        p_cur = jnp.exp2(sc_ref[head] - m).astype(BF16)
        p_prev = jnp.exp2(sp_ref[head] - m).astype(BF16)
        acc = (jnp.dot(v_cur, p_cur, preferred_element_type=F32)
               + jnp.dot(v_prev, p_prev, preferred_element_type=F32))
        den = acc[HEAD_DIM:HEAD_DIM + 1] + jnp.exp2(sink_ref[head] * LOG2E - m)
        ot_ref[head * HEAD_DIM:(head + 1) * HEAD_DIM, :] = acc[:HEAD_DIM] / den
    o_ref[0] = ot_ref[...].T.astype(BF16)


def _swa(sinks, qt_a, k_a, vt_a, swa_bias):
    B, nblk = qt_a.shape[0], qt_a.shape[1]
    S = nblk * T
    prev = lambda b, i: (b, jnp.maximum(i - 1, 0), 0)
    prev4 = lambda b, i: (b, jnp.maximum(i - 1, 0), 0, 0)
    return pl.pallas_call(
        _swa_kernel,
        grid=(B, nblk),
        in_specs=[
            pl.BlockSpec(memory_space=pltpu.SMEM),
            pl.BlockSpec((1, 1, SWA_WIDTH, T), lambda b, i: (b, i, 0, 0)),
            pl.BlockSpec((1, T, SWA_KV_WIDTH), lambda b, i: (b, i, 0)),
            pl.BlockSpec((1, T, SWA_KV_WIDTH), prev),
            pl.BlockSpec((1, 1, SWA_KV_WIDTH, T), lambda b, i: (b, i, 0, 0)),
            pl.BlockSpec((1, 1, SWA_KV_WIDTH, T), prev4),
            pl.BlockSpec((SWA_Q_HEADS, 2, T, T), lambda b, i: (0, 0, 0, 0)),
        ],
        out_specs=pl.BlockSpec((1, T, SWA_WIDTH), lambda b, i: (b, i, 0)),
        out_shape=jax.ShapeDtypeStruct((B, S, SWA_WIDTH), BF16),
        scratch_shapes=[pltpu.VMEM((SWA_WIDTH, T), F32),
                        pltpu.VMEM((SWA_Q_HEADS, T, T), F32),
                        pltpu.VMEM((SWA_Q_HEADS, T // 2, T), F32)],
        compiler_params=pltpu.CompilerParams(
            dimension_semantics=("arbitrary", "arbitrary"), vmem_limit_bytes=VMEM_LIMIT),
        name="swa_attention",
    )(sinks, qt_a, k_a, k_a, vt_a, vt_a, swa_bias)


def _moba_kernel(qt_ref, k_ref, vt_ref, bias_ref, o_ref,
                 kaug_ref, vaug_ref, kmean_ref, qaug_ref, ot_ref,
                 s00_ref, s01_ref, s10_ref, s11_ref, *, nblk):
    s_refs = ((s00_ref, s01_ref), (s10_ref, s11_ref))
    heads = range(2)
    head_rows = [slice(hl * HEAD_DIM, (hl + 1) * HEAD_DIM) for hl in heads]

    rows = KEY_GROUP * T
    sel_r = lax.broadcasted_iota(jnp.int32, (2 * HEAD_DIM, AUG), 0)
    sel_c = lax.broadcasted_iota(jnp.int32, (2 * HEAD_DIM, AUG), 1)
    picks = [jnp.where((sel_r == sel_c + hl * HEAD_DIM) & (sel_c < HEAD_DIM), 1.0, 0.0).astype(BF16)
             for hl in heads]
    lane = lax.broadcasted_iota(jnp.int32, (rows, AUG), 1)
    row_blk = lax.shift_right_logical(lax.broadcasted_iota(jnp.int32, (rows, AUG), 0), T.bit_length() - 1)
    pad = _ones_pad_rows(T)

    def prep(c, carry):
        row0 = pl.multiple_of(c * rows, rows)
        blk0 = pl.multiple_of(c * KEY_GROUP, KEY_GROUP)
        kb = k_ref[0, pl.ds(row0, rows), :]
        onehot = jnp.where(lane == HEAD_DIM + blk0 + row_blk, 1.0, 0.0)
        for hl in heads:
            kp = jnp.dot(kb, picks[hl], preferred_element_type=F32)
            kaug_ref[hl, pl.ds(row0, rows), :] = (kp + onehot).astype(BF16)
            kmean_ref[hl, pl.ds(blk0, KEY_GROUP), :] = (
                jnp.sum(kp.reshape(KEY_GROUP, T, AUG), axis=1) * (1.0 / T))
            for r in range(KEY_GROUP):
                vaug_ref[hl, blk0 + r] = jnp.concatenate([vt_ref[0, blk0 + r, head_rows[hl], :], pad], axis=0)
        return carry

    lax.fori_loop(0, nblk // KEY_GROUP, prep, 0)

    width = QUERY_GROUP * T
    blk_id = lax.broadcasted_iota(jnp.int32, (nblk, width), 0)
    lane_blk = lax.shift_right_logical(lax.broadcasted_iota(jnp.int32, (nblk, width), 1), T.bit_length() - 1)
    zeros_q = jnp.zeros((AUG - HEAD_DIM, T), BF16)
    zeros_tail = jnp.zeros((AUG - HEAD_DIM - nblk, T), BF16)
    kmeans = []
    for hl in heads:
        km = kmean_ref[hl]
        km_hi = km.astype(BF16)
        kmeans.append((km_hi, (km - km_hi.astype(F32)).astype(BF16)))

    def route(c, carry):
        blk0 = pl.multiple_of(c * QUERY_GROUP, QUERY_GROUP)
        q_blk = blk0 + lane_blk
        for hl in heads:
            km_hi, km_lo = kmeans[hl]
            qts = [qt_ref[0, blk0 + x, head_rows[hl], :] for x in range(QUERY_GROUP)]
            scs = []
            for qt in qts:
                qs = jnp.concatenate([qt, zeros_q], axis=0)
                scs.append(jnp.dot(km_hi, qs, preferred_element_type=F32)
                           + jnp.dot(km_lo, qs, preferred_element_type=F32))
            sc = jnp.concatenate(scs, axis=1)
            sc = jnp.where(blk_id < q_blk, sc, -jnp.inf)
            chosen = blk_id == q_blk
            for _ in range(MOBA_TOPK):
                mx = jnp.max(sc, axis=0, keepdims=True)
                first = jnp.min(jnp.where(sc == mx, blk_id, nblk), axis=0, keepdims=True)
                hit = (blk_id == first) & (mx > -jnp.inf)
                chosen = chosen | hit
                sc = jnp.where(hit, -jnp.inf, sc)
            sel_bias = jnp.where(chosen, 0.0, NEG).astype(BF16)
            for x in range(QUERY_GROUP):
                qaug_ref[hl, blk0 + x] = jnp.concatenate(
                    [qts[x], sel_bias[:, x * T:(x + 1) * T], zeros_tail], axis=0)
        return carry

    lax.fori_loop(0, nblk // QUERY_GROUP, route, 0)

    def q_block(i, carry):
        npairs = (i + 2) // 2

        def score(hl, slot, j, m):
            row0 = pl.multiple_of(j * T, T)
            st = jnp.dot(kaug_ref[hl, pl.ds(row0, T), :], qaug_ref[hl, i], preferred_element_type=F32)
            st = st + bias_ref[hl, jnp.clip(i - j, 0, N_NEAR)]
            s_refs[hl][slot][...] = st
            m_new = jnp.maximum(m, jnp.max(st, axis=0, keepdims=True))
            return m_new, jnp.exp2(m - m_new)

        def accumulate(hl, slot, j, m, alpha, acc):
            p = jnp.exp2(s_refs[hl][slot][...] - m).astype(BF16)
            return acc * alpha + jnp.dot(vaug_ref[hl, j], p, preferred_element_type=F32)

        def half_step(state, slot, j, last=False):
            if last:
                return [accumulate(hl, slot, j, *state[hl]) for hl in range(2)]
            nxt = [score(hl, 1 - slot, j + 1, state[hl][0]) for hl in range(2)]
            return tuple(nxt[hl] + (accumulate(hl, slot, j, *state[hl]),) for hl in range(2))

        m0 = jnp.full((1, T), NEG, F32)
        state = tuple(score(hl, 0, 0, m0) + (jnp.zeros((V_ROWS, T), F32),) for hl in range(2))

        def kv_pair(u, state):
            state = half_step(state, 0, 2 * u)
            return half_step(state, 1, 2 * u + 1)

        state = lax.fori_loop(0, npairs - 1, kv_pair, state)
        j_last = 2 * (npairs - 1)
        accs = half_step(half_step(state, 0, j_last), 1, j_last + 1, last=True)
        for hl in range(2):
            acc = accs[hl]
            ot_ref[hl * HEAD_DIM:(hl + 1) * HEAD_DIM, :] = acc[:HEAD_DIM] / acc[HEAD_DIM:HEAD_DIM + 1]
        o_ref[0, pl.ds(pl.multiple_of(i * T, T), T), :] = ot_ref[...].T.astype(BF16)
        return carry

    lax.fori_loop(0, nblk, q_block, 0)


def _moba(qt_b, k_b, vt_b, moba_bias):
    B, nblk = qt_b.shape[0], qt_b.shape[1]
    S = nblk * T
    assert HEAD_DIM + nblk <= AUG, "block one-hot must fit the augmented contraction width"
    assert nblk % KEY_GROUP == 0 and nblk % QUERY_GROUP == 0 and nblk % 2 == 0
    pair = 2 * HEAD_DIM
    return pl.pallas_call(
        functools.partial(_moba_kernel, nblk=nblk),
        grid=(B, MOBA_HEADS // 2),
        in_specs=[
            pl.BlockSpec((1, nblk, pair, T), lambda b, h: (b, 0, h, 0)),
            pl.BlockSpec((1, S, pair), lambda b, h: (b, 0, h)),
            pl.BlockSpec((1, nblk, pair, T), lambda b, h: (b, 0, h, 0)),
            pl.BlockSpec((2, N_NEAR + 1, T, T), lambda b, h: (h, 0, 0, 0)),
        ],
        out_specs=pl.BlockSpec((1, S, pair), lambda b, h: (b, 0, h)),
        out_shape=jax.ShapeDtypeStruct((B, S, MOBA_WIDTH), BF16),
        scratch_shapes=[
            pltpu.VMEM((2, S, AUG), BF16),
            pltpu.VMEM((2, nblk, V_ROWS, T), BF16),
            pltpu.VMEM((2, nblk, AUG), F32),
            pltpu.VMEM((2, nblk, AUG, T), BF16),
            pltpu.VMEM((pair, T), F32),
        ] + [pltpu.VMEM((T, T), F32)] * 4,
        compiler_params=pltpu.CompilerParams(
            dimension_semantics=("arbitrary", "arbitrary"), vmem_limit_bytes=VMEM_LIMIT),
        name="moba_attention",
    )(qt_b, k_b, vt_b, moba_bias)


def _sigmoid(v):
    return 1.0 / (1.0 + jnp.exp(-v))


def _rms(v, gain):
    r = lax.rsqrt(jnp.mean(v * v, axis=-1, keepdims=True) + RMS_EPS)
    return (v * r) * gain


def _outproj_kernel(x_ref, ao_ref, bo_ref, ga_ref, gb_ref, p_ref,
                    wo_ref, pg_ref, wg_ref, wp_ref, fg_ref, o_ref, *, final):
    ga = ga_ref[0].astype(F32)
    gb = gb_ref[0].astype(F32)
    ma = (ao_ref[0].astype(F32) * (ga * _sigmoid(ga))).astype(BF16)
    mb = (bo_ref[0].astype(F32) * (gb * _sigmoid(gb))).astype(BF16)
    x1 = (x_ref[0]
          + jnp.dot(ma, wo_ref[0:SWA_WIDTH, :], preferred_element_type=F32)
          + jnp.dot(mb, wo_ref[SWA_WIDTH:, :], preferred_element_type=F32))
    n = _rms(x1, pg_ref[...]).astype(BF16)
    gate = _sigmoid(jnp.dot(n, wg_ref[...], preferred_element_type=F32))
    up = jnp.dot(p_ref[0].astype(BF16), wp_ref[...], preferred_element_type=F32)
    y = x1 + up * gate
    if final:
        y = _rms(y, fg_ref[...])
    o_ref[0] = y


def _outproj(x, a_out, b_out, g_a, g_b, p, w_out, ple_gain, w_gate, w_proj, final_gain, final):
    B, S, D = x.shape
    nt = S // ROW_TILE
    ple = p.shape[-1]

    def tok(width):
        return pl.BlockSpec((1, ROW_TILE, width), lambda b, t: (b, t, 0))

    def whole(shape):
        return pl.BlockSpec(shape, lambda b, t: (0,) * len(shape))

    return pl.pallas_call(
        functools.partial(_outproj_kernel, final=final),
        grid=(B, nt),
        in_specs=[tok(D), tok(SWA_WIDTH), tok(MOBA_WIDTH), tok(SWA_WIDTH), tok(MOBA_WIDTH), tok(ple),
                  whole((D, D)), whole((1, D)), whole((D, D)), whole((ple, D)), whole((1, D))],
        out_specs=tok(D),
        out_shape=jax.ShapeDtypeStruct((B, S, D), F32),
        compiler_params=pltpu.CompilerParams(
            dimension_semantics=("arbitrary", "arbitrary"), vmem_limit_bytes=VMEM_LIMIT),
        name="outproj_ple",
    )(x, a_out, b_out, g_a, g_b, p, w_out, ple_gain, w_gate, w_proj, final_gain)


def _split_in_weights(w):
    widths = (SWA_WIDTH, SWA_KV_WIDTH, SWA_KV_WIDTH, SWA_WIDTH,
              MOBA_WIDTH, MOBA_WIDTH, MOBA_WIDTH, MOBA_WIDTH)
    offs = [0]
    for wd in widths:
        offs.append(offs[-1] + wd)
    a_q, a_k, a_v, a_g, b_q, b_k, b_v, b_g = (w[:, offs[n]:offs[n + 1]] for n in range(8))
    scale = HEAD_DIM ** -0.5 * LOG2E
    w_nn = jnp.concatenate([a_k, a_g, b_k, b_g], axis=1).astype(BF16)
    w_t = jnp.concatenate([a_q * scale, a_v, b_q * scale, b_v], axis=1).T.astype(BF16)
    return w_nn, w_t


def kernel(x, p, norm_in, w_in, sinks, rel_bias, w_out, ple_norm, w_ple_gate, w_ple_proj, final_norm):
    B, S, D = x.shape
    depth = p.shape[0]
    assert S % ROW_TILE == 0 and ROW_TILE % T == 0
    moba_bias, swa_bias = _bias_tables(rel_bias)
    for i in range(depth):
        w_nn, w_t = _split_in_weights(w_in[i])
        k_a, g_a, k_b, g_b, qt_a, vt_a, qt_b, vt_b = _inproj(x, norm_in[i][None, :], w_nn, w_t)
        a_out = _swa(sinks[i], qt_a, k_a, vt_a, swa_bias)
        b_out = _moba(qt_b, k_b, vt_b, moba_bias)
        x = _outproj(x, a_out, b_out, g_a, g_b, p[i],
                     w_out[i].astype(BF16), ple_norm[i][None, :], w_ple_gate[i].astype(BF16),
                     w_ple_proj[i].astype(BF16), final_norm[None, :], final=(i == depth - 1))
    return x
```

```python
import functools
import math

import jax
import jax.numpy as jnp
from jax import lax
from jax.experimental import pallas as pl
from jax.experimental.pallas import tpu as pltpu

F32 = jnp.float32
BF16 = jnp.bfloat16

HEAD_DIM = 64
SWA_Q_HEADS = 8
SWA_KV_HEADS = 2
MOBA_HEADS = 8
WINDOW = 128
MOBA_BLOCK = 256
MOBA_TOPK = 3
NUM_BUCKETS = 32
MAX_DISTANCE = 1024
RMS_EPS = 1e-6

SWA_WIDTH = SWA_Q_HEADS * HEAD_DIM
SWA_KV_WIDTH = SWA_KV_HEADS * HEAD_DIM
MOBA_WIDTH = MOBA_HEADS * HEAD_DIM
SWA_GROUP = SWA_Q_HEADS // SWA_KV_HEADS

T = MOBA_BLOCK
AUG = 128
V_ROWS = HEAD_DIM + 16
NEG = -1e30
LOG2E = math.log2(math.e)
N_NEAR = -(-(MAX_DISTANCE + T) // T)
KEY_GROUP = 8
QUERY_GROUP = 4
assert WINDOW <= T // 2
ROW_TILE = 512
VMEM_LIMIT = 56 * 1024 * 1024


def _rel_bucket(dist):
    n = jnp.maximum(dist, 0)
    max_exact = NUM_BUCKETS // 2
    nf = jnp.maximum(n, 1).astype(F32)
    large = max_exact + (jnp.log(nf / max_exact) / math.log(MAX_DISTANCE / max_exact)
                         * (NUM_BUCKETS - max_exact)).astype(jnp.int32)
    large = jnp.minimum(large, NUM_BUCKETS - 1)
    return jnp.where(n < max_exact, n, large)


def _bias_kernel(rel_ref, moba_ref, swa_ref):
    h = pl.program_id(0)
    kk = lax.broadcasted_iota(jnp.int32, (T, T), 0)
    qq = lax.broadcasted_iota(jnp.int32, (T, T), 1)
    rel = qq - kk

    def lookup(bucket, col):
        acc = jnp.zeros((T, T), F32)
        for b in range(NUM_BUCKETS):
            acc = jnp.where(bucket == b, rel_ref[b, col] * LOG2E, acc)
        return acc

    for d in range(N_NEAR):
        dist = rel + d * T
        bias = lookup(_rel_bucket(dist), SWA_Q_HEADS + h)
        if d == 0:
            bias = jnp.where(dist >= 0, bias, NEG)
        moba_ref[0, d] = bias
    far = jnp.full((T, T), N_NEAR * T, jnp.int32)
    moba_ref[0, N_NEAR] = lookup(_rel_bucket(far), SWA_Q_HEADS + h)

    for d in range(2):
        dist = rel + d * T
        bias = lookup(_rel_bucket(dist), h)
        valid = (dist >= 0) & (dist < WINDOW)
        swa_ref[0, d] = jnp.where(valid, bias, NEG)


def _bias_tables(rel_bias):
    return pl.pallas_call(
        _bias_kernel,
        grid=(MOBA_HEADS,),
        in_specs=[pl.BlockSpec(memory_space=pltpu.SMEM)],
        out_specs=[
            pl.BlockSpec((1, N_NEAR + 1, T, T), lambda h: (h, 0, 0, 0)),
            pl.BlockSpec((1, 2, T, T), lambda h: (h, 0, 0, 0)),
        ],
        out_shape=[
            jax.ShapeDtypeStruct((MOBA_HEADS, N_NEAR + 1, T, T), F32),
            jax.ShapeDtypeStruct((SWA_Q_HEADS, 2, T, T), F32),
        ],
        name="bias_tables",
    )(rel_bias)


def _inproj_kernel(x_ref, g_ref, wnn_ref, wt_ref,
                   ka_ref, ga_ref, kb_ref, gb_ref,
                   qta_ref, vta_ref, qtb_ref, vtb_ref):
    x = x_ref[0]
    r = lax.rsqrt(jnp.mean(x * x, axis=-1, keepdims=True) + RMS_EPS)
    h = ((x * r) * g_ref[...]).astype(BF16)

    tok = jnp.dot(h, wnn_ref[...], preferred_element_type=F32)
    o = 0
    for ref, width in ((ka_ref, SWA_KV_WIDTH), (ga_ref, SWA_WIDTH),
                       (kb_ref, MOBA_WIDTH), (gb_ref, MOBA_WIDTH)):
        ref[0] = tok[:, o:o + width].astype(BF16)
        o += width

    feat = lax.dot_general(wt_ref[...], h, (((1,), (1,)), ((), ())),
                           preferred_element_type=F32)
    o = 0
    for ref, width in ((qta_ref, SWA_WIDTH), (vta_ref, SWA_KV_WIDTH),
                       (qtb_ref, MOBA_WIDTH), (vtb_ref, MOBA_WIDTH)):
        for c in range(ROW_TILE // T):
            ref[0, c] = feat[o:o + width, c * T:(c + 1) * T].astype(BF16)
        o += width


def _inproj(x, gain, w_nn, w_t):
    B, S, D = x.shape
    nt = S // ROW_TILE
    cpt = ROW_TILE // T
    n_nn = w_nn.shape[1]
    n_t = w_t.shape[0]

    def tok_spec(width):
        return pl.BlockSpec((1, ROW_TILE, width), lambda b, t: (b, t, 0))

    def feat_spec(width):
        return pl.BlockSpec((1, cpt, width, T), lambda b, t: (b, t, 0, 0))

    def tok_shape(width):
        return jax.ShapeDtypeStruct((B, S, width), BF16)

    def feat_shape(width):
        return jax.ShapeDtypeStruct((B, S // T, width, T), BF16)

    return pl.pallas_call(
        _inproj_kernel,
        grid=(B, nt),
        in_specs=[
            pl.BlockSpec((1, ROW_TILE, D), lambda b, t: (b, t, 0)),
            pl.BlockSpec((1, D), lambda b, t: (0, 0)),
            pl.BlockSpec((D, n_nn), lambda b, t: (0, 0)),
            pl.BlockSpec((n_t, D), lambda b, t: (0, 0)),
        ],
        out_specs=[tok_spec(SWA_KV_WIDTH), tok_spec(SWA_WIDTH), tok_spec(MOBA_WIDTH), tok_spec(MOBA_WIDTH),
                   feat_spec(SWA_WIDTH), feat_spec(SWA_KV_WIDTH), feat_spec(MOBA_WIDTH), feat_spec(MOBA_WIDTH)],
        out_shape=[tok_shape(SWA_KV_WIDTH), tok_shape(SWA_WIDTH), tok_shape(MOBA_WIDTH), tok_shape(MOBA_WIDTH),
                   feat_shape(SWA_WIDTH), feat_shape(SWA_KV_WIDTH), feat_shape(MOBA_WIDTH), feat_shape(MOBA_WIDTH)],
        compiler_params=pltpu.CompilerParams(
            dimension_semantics=("arbitrary", "arbitrary"), vmem_limit_bytes=VMEM_LIMIT),
        name="norm_inproj",
    )(x, gain, w_nn, w_t)


def _ones_pad_rows(width):
    row = lax.broadcasted_iota(jnp.int32, (V_ROWS - HEAD_DIM, width), 0)
    return jnp.where(row == 0, 1.0, 0.0).astype(BF16)


def _swa_kernel(sink_ref, qt_ref, kcur_ref, kprev_ref, vcur_ref, vprev_ref, bias_ref,
                o_ref, ot_ref, sc_ref, sp_ref):
    blk = pl.program_id(1)
    half = T // 2
    k_cur = kcur_ref[0]
    k_prev = kprev_ref[0, half:, :]
    zeros_q = jnp.zeros((HEAD_DIM, T), BF16)
    prev_pen = jnp.where(blk == 0, NEG, 0.0).astype(F32)

    maxes = []
    for head in range(SWA_Q_HEADS):
        g = head // SWA_GROUP
        qt = qt_ref[0, 0, head * HEAD_DIM:(head + 1) * HEAD_DIM, :]
        parts = [zeros_q] * SWA_KV_HEADS
        parts[g] = qt
        q_pad = jnp.concatenate(parts, axis=0)
        s_cur = jnp.dot(k_cur, q_pad, preferred_element_type=F32) + bias_ref[head, 0]
        s_prev = jnp.dot(k_prev, q_pad, preferred_element_type=F32) + (bias_ref[head, 1, half:, :] + prev_pen)
        sc_ref[head] = s_cur
        sp_ref[head] = s_prev
        m = jnp.maximum(jnp.max(s_cur, axis=0, keepdims=True), jnp.max(s_prev, axis=0, keepdims=True))
        maxes.append(jnp.maximum(m, sink_ref[head] * LOG2E))

    for head in range(SWA_Q_HEADS):
        g = head // SWA_GROUP
        rows = slice(g * HEAD_DIM, (g + 1) * HEAD_DIM)
        v_cur = jnp.concatenate([vcur_ref[0, 0, rows, :], _ones_pad_rows(T)], axis=0)
        v_prev = jnp.concatenate([vprev_ref[0, 0, rows, half:], _ones_pad_rows(half)], axis=0)
        m = maxes[head]
        p_cur = jnp.exp2(sc_ref[head] - m).astype(BF16)
        p_prev = jnp.exp2(sp_ref[head] - m).astype(BF16)
        acc = (jnp.dot(v_cur, p_cur, preferred_element_type=F32)
               + jnp.dot(v_prev, p_prev, preferred_element_type=F32))
        den = acc[HEAD_DIM:HEAD_DIM + 1] + jnp.exp2(sink_ref[head] * LOG2E - m)
        ot_ref[head * HEAD_DIM:(head + 1) * HEAD_DIM, :] = acc[:HEAD_DIM] / den
    o_ref[0] = ot_ref[...].T.astype(BF16)


def _swa(sinks, qt_a, k_a, vt_a, swa_bias):
    B, nblk = qt_a.shape[0], qt_a.shape[1]
    S = nblk * T
    prev = lambda b, i: (b, jnp.maximum(i - 1, 0), 0)
    prev4 = lambda b, i: (b, jnp.maximum(i - 1, 0), 0, 0)
    return pl.pallas_call(
        _swa_kernel,
        grid=(B, nblk),
        in_specs=[
            pl.BlockSpec(memory_space=pltpu.SMEM),
            pl.BlockSpec((1, 1, SWA_WIDTH, T), lambda b, i: (b, i, 0, 0)),
            pl.BlockSpec((1, T, SWA_KV_WIDTH), lambda b, i: (b, i, 0)),
            pl.BlockSpec((1, T, SWA_KV_WIDTH), prev),
            pl.BlockSpec((1, 1, SWA_KV_WIDTH, T), lambda b, i: (b, i, 0, 0)),
            pl.BlockSpec((1, 1, SWA_KV_WIDTH, T), prev4),
            pl.BlockSpec((SWA_Q_HEADS, 2, T, T), lambda b, i: (0, 0, 0, 0)),
        ],
        out_specs=pl.BlockSpec((1, T, SWA_WIDTH), lambda b, i: (b, i, 0)),
        out_shape=jax.ShapeDtypeStruct((B, S, SWA_WIDTH), BF16),
        scratch_shapes=[pltpu.VMEM((SWA_WIDTH, T), F32),
                        pltpu.VMEM((SWA_Q_HEADS, T, T), F32),
                        pltpu.VMEM((SWA_Q_HEADS, T // 2, T), F32)],
        compiler_params=pltpu.CompilerParams(
            dimension_semantics=("arbitrary", "arbitrary"), vmem_limit_bytes=VMEM_LIMIT),
        name="swa_attention",
    )(sinks, qt_a, k_a, k_a, vt_a, vt_a, swa_bias)


def _moba_kernel(qt_ref, k_ref, vt_ref, bias_ref, o_ref,
                 kaug_ref, vaug_ref, kmean_ref, qaug_ref, acc_ref, ot_ref,
                 s00_ref, s01_ref, s10_ref, s11_ref, *, nblk):
    s_refs = ((s00_ref, s01_ref), (s10_ref, s11_ref))
    heads = range(2)
    head_rows = [slice(hl * HEAD_DIM, (hl + 1) * HEAD_DIM) for hl in heads]

    rows = KEY_GROUP * T
    sel_r = lax.broadcasted_iota(jnp.int32, (2 * HEAD_DIM, AUG), 0)
    sel_c = lax.broadcasted_iota(jnp.int32, (2 * HEAD_DIM, AUG), 1)
    picks = [jnp.where((sel_r == sel_c + hl * HEAD_DIM) & (sel_c < HEAD_DIM), 1.0, 0.0).astype(BF16)
             for hl in heads]
    lane = lax.broadcasted_iota(jnp.int32, (rows, AUG), 1)
    row_blk = lax.shift_right_logical(lax.broadcasted_iota(jnp.int32, (rows, AUG), 0), T.bit_length() - 1)
    pad = _ones_pad_rows(T)

    def prep(c, carry):
        row0 = pl.multiple_of(c * rows, rows)
        blk0 = pl.multiple_of(c * KEY_GROUP, KEY_GROUP)
        kb = k_ref[0, pl.ds(row0, rows), :]
        onehot = jnp.where(lane == HEAD_DIM + blk0 + row_blk, 1.0, 0.0)
        for hl in heads:
            kp = jnp.dot(kb, picks[hl], preferred_element_type=F32)
            kaug_ref[hl, pl.ds(row0, rows), :] = (kp + onehot).astype(BF16)
            kmean_ref[hl, pl.ds(blk0, KEY_GROUP), :] = (
                jnp.sum(kp.reshape(KEY_GROUP, T, AUG), axis=1) * (1.0 / T))
            for r in range(KEY_GROUP):
                vaug_ref[hl, c * (KEY_GROUP // 2) + r // 2, :, (r % 2) * T:(r % 2 + 1) * T] = (
                    jnp.concatenate([vt_ref[0, blk0 + r, head_rows[hl], :], pad], axis=0))
        return carry

    lax.fori_loop(0, nblk // KEY_GROUP, prep, 0)

    width = QUERY_GROUP * T
    blk_id = lax.broadcasted_iota(jnp.int32, (nblk, width), 0)
    lane_blk = lax.shift_right_logical(lax.broadcasted_iota(jnp.int32, (nblk, width), 1), T.bit_length() - 1)
    zeros_q = jnp.zeros((AUG - HEAD_DIM, T), BF16)
    zeros_tail = jnp.zeros((AUG - HEAD_DIM - nblk, T), BF16)
    kmeans = []
    for hl in heads:
        km = kmean_ref[hl]
        km_hi = km.astype(BF16)
        kmeans.append((km_hi, (km - km_hi.astype(F32)).astype(BF16)))

    def route(c, carry):
        blk0 = pl.multiple_of(c * QUERY_GROUP, QUERY_GROUP)
        q_blk = blk0 + lane_blk
        for hl in heads:
            km_hi, km_lo = kmeans[hl]
            qts = [qt_ref[0, blk0 + x, head_rows[hl], :] for x in range(QUERY_GROUP)]
            scs = []
            for qt in qts:
                qs = jnp.concatenate([qt, zeros_q], axis=0)
                scs.append(jnp.dot(km_hi, qs, preferred_element_type=F32)
                           + jnp.dot(km_lo, qs, preferred_element_type=F32))
            sc = jnp.concatenate(scs, axis=1)
            sc = jnp.where(blk_id < q_blk, sc, -jnp.inf)
            chosen = blk_id == q_blk
            for _ in range(MOBA_TOPK):
                mx = jnp.max(sc, axis=0, keepdims=True)
                first = jnp.min(jnp.where(sc == mx, blk_id, nblk), axis=0, keepdims=True)
                hit = (blk_id == first) & (mx > -jnp.inf)
                chosen = chosen | hit
                sc = jnp.where(hit, -jnp.inf, sc)
            sel_bias = jnp.where(chosen, 0.0, NEG).astype(BF16)
            for x in range(QUERY_GROUP):
                qaug_ref[hl, blk0 + x] = jnp.concatenate(
                    [qts[x], sel_bias[:, x * T:(x + 1) * T], zeros_tail], axis=0)
        return carry

    lax.fori_loop(0, nblk // QUERY_GROUP, route, 0)

    def next_item(i, u):
        last = u + 1 == (i + 2) // 2
        return jnp.where(last, i + 1, i), jnp.where(last, 0, u + 1)

    def score(hl, slot, i, u, m):
        i = jnp.minimum(i, nblk - 1)
        row0 = pl.multiple_of(u * 2 * T, 2 * T)
        st = jnp.dot(kaug_ref[hl, pl.ds(row0, 2 * T), :], qaug_ref[hl, i], preferred_element_type=F32)
        s_even = st[0:T] + bias_ref[hl, jnp.clip(i - 2 * u, 0, N_NEAR)]
        s_odd = st[T:] + bias_ref[hl, jnp.clip(i - 2 * u - 1, 0, N_NEAR)]
        s_refs[hl][slot][0:T, :] = s_even
        s_refs[hl][slot][T:, :] = s_odd
        m = jnp.where(u == 0, NEG, m)
        top = jnp.maximum(jnp.max(s_even, axis=0, keepdims=True), jnp.max(s_odd, axis=0, keepdims=True))
        m_new = jnp.maximum(m, top)
        return m_new, jnp.exp2(m - m_new)

    def accumulate(hl, slot, i, u, m, alpha, acc):
        p = jnp.exp2(s_refs[hl][slot][...] - m).astype(BF16)
        acc = acc * alpha + jnp.dot(vaug_ref[hl, u], p, preferred_element_type=F32)
        acc_ref[hl, i] = acc
        return acc

    def item_step(slot, i, u, state):
        i_nxt, u_nxt = next_item(i, u)
        nxt = [score(hl, 1 - slot, i_nxt, u_nxt, state[hl][0]) for hl in heads]
        state = tuple(nxt[hl] + (accumulate(hl, slot, i, u, *state[hl]),) for hl in heads)
        return i_nxt, u_nxt, state

    def two_items(_, carry):
        i, u, state = carry
        i, u, state = item_step(0, i, u, state)
        return item_step(1, i, u, state)

    zero = jnp.int32(0)
    m0 = jnp.full((1, T), NEG, F32)
    state = tuple(score(hl, 0, zero, zero, m0) + (jnp.zeros((V_ROWS, T), F32),) for hl in heads)
    n_items = sum((i + 2) // 2 for i in range(nblk))
    assert n_items % 2 == 0
    lax.fori_loop(0, n_items // 2, two_items, (zero, zero, state))

    def finish(i, carry):
        for hl in heads:
            acc = acc_ref[hl, i]
            ot_ref[head_rows[hl], :] = acc[:HEAD_DIM] / acc[HEAD_DIM:HEAD_DIM + 1]
        o_ref[0, pl.ds(pl.multiple_of(i * T, T), T), :] = ot_ref[...].T.astype(BF16)
        return carry

    lax.fori_loop(0, nblk, finish, 0)


def _moba(qt_b, k_b, vt_b, moba_bias):
    B, nblk = qt_b.shape[0], qt_b.shape[1]
    S = nblk * T
    assert HEAD_DIM + nblk <= AUG, "block one-hot must fit the augmented contraction width"
    assert nblk % KEY_GROUP == 0 and nblk % QUERY_GROUP == 0 and nblk % 2 == 0
    pair = 2 * HEAD_DIM
    return pl.pallas_call(
        functools.partial(_moba_kernel, nblk=nblk),
        grid=(B, MOBA_HEADS // 2),
        in_specs=[
            pl.BlockSpec((1, nblk, pair, T), lambda b, h: (b, 0, h, 0)),
            pl.BlockSpec((1, S, pair), lambda b, h: (b, 0, h)),
            pl.BlockSpec((1, nblk, pair, T), lambda b, h: (b, 0, h, 0)),
            pl.BlockSpec((2, N_NEAR + 1, T, T), lambda b, h: (h, 0, 0, 0)),
        ],
        out_specs=pl.BlockSpec((1, S, pair), lambda b, h: (b, 0, h)),
        out_shape=jax.ShapeDtypeStruct((B, S, MOBA_WIDTH), BF16),
        scratch_shapes=[
            pltpu.VMEM((2, S, AUG), BF16),
            pltpu.VMEM((2, nblk // 2, V_ROWS, 2 * T), BF16),
            pltpu.VMEM((2, nblk, AUG), F32),
            pltpu.VMEM((2, nblk, AUG, T), BF16),
            pltpu.VMEM((2, nblk, V_ROWS, T), F32),
            pltpu.VMEM((pair, T), F32),
        ] + [pltpu.VMEM((2 * T, T), F32)] * 4,
        compiler_params=pltpu.CompilerParams(
            dimension_semantics=("arbitrary", "arbitrary"), vmem_limit_bytes=VMEM_LIMIT),
        name="moba_attention",
    )(qt_b, k_b, vt_b, moba_bias)


def _sigmoid(v):
    return 1.0 / (1.0 + jnp.exp(-v))


def _rms(v, gain):
    r = lax.rsqrt(jnp.mean(v * v, axis=-1, keepdims=True) + RMS_EPS)
    return (v * r) * gain


def _outproj_kernel(x_ref, ao_ref, bo_ref, ga_ref, gb_ref, p_ref,
                    wo_ref, pg_ref, wg_ref, wp_ref, fg_ref, o_ref, *, final):
    ga = ga_ref[0].astype(F32)
    gb = gb_ref[0].astype(F32)
    ma = (ao_ref[0].astype(F32) * (ga * _sigmoid(ga))).astype(BF16)
    mb = (bo_ref[0].astype(F32) * (gb * _sigmoid(gb))).astype(BF16)
    x1 = (x_ref[0]
          + jnp.dot(ma, wo_ref[0:SWA_WIDTH, :], preferred_element_type=F32)
          + jnp.dot(mb, wo_ref[SWA_WIDTH:, :], preferred_element_type=F32))
    n = _rms(x1, pg_ref[...]).astype(BF16)
    gate = _sigmoid(jnp.dot(n, wg_ref[...], preferred_element_type=F32))
    up = jnp.dot(p_ref[0].astype(BF16), wp_ref[...], preferred_element_type=F32)
    y = x1 + up * gate
    if final:
        y = _rms(y, fg_ref[...])
    o_ref[0] = y


def _outproj(x, a_out, b_out, g_a, g_b, p, w_out, ple_gain, w_gate, w_proj, final_gain, final):
    B, S, D = x.shape
    nt = S // ROW_TILE
    ple = p.shape[-1]

    def tok(width):
        return pl.BlockSpec((1, ROW_TILE, width), lambda b, t: (b, t, 0))

    def whole(shape):
        return pl.BlockSpec(shape, lambda b, t: (0,) * len(shape))

    return pl.pallas_call(
        functools.partial(_outproj_kernel, final=final),
        grid=(B, nt),
        in_specs=[tok(D), tok(SWA_WIDTH), tok(MOBA_WIDTH), tok(SWA_WIDTH), tok(MOBA_WIDTH), tok(ple),
                  whole((D, D)), whole((1, D)), whole((D, D)), whole((ple, D)), whole((1, D))],
        out_specs=tok(D),
        out_shape=jax.ShapeDtypeStruct((B, S, D), F32),
        compiler_params=pltpu.CompilerParams(
            dimension_semantics=("arbitrary", "arbitrary"), vmem_limit_bytes=VMEM_LIMIT),
        name="outproj_ple",
    )(x, a_out, b_out, g_a, g_b, p, w_out, ple_gain, w_gate, w_proj, final_gain)


def _split_in_weights(w):
    widths = (SWA_WIDTH, SWA_KV_WIDTH, SWA_KV_WIDTH, SWA_WIDTH,
              MOBA_WIDTH, MOBA_WIDTH, MOBA_WIDTH, MOBA_WIDTH)
    offs = [0]
    for wd in widths:
        offs.append(offs[-1] + wd)
    a_q, a_k, a_v, a_g, b_q, b_k, b_v, b_g = (w[:, offs[n]:offs[n + 1]] for n in range(8))
    scale = HEAD_DIM ** -0.5 * LOG2E
    w_nn = jnp.concatenate([a_k, a_g, b_k, b_g], axis=1).astype(BF16)
    w_t = jnp.concatenate([a_q * scale, a_v, b_q * scale, b_v], axis=1).T.astype(BF16)
    return w_nn, w_t


def kernel(x, p, norm_in, w_in, sinks, rel_bias, w_out, ple_norm, w_ple_gate, w_ple_proj, final_norm):
    B, S, D = x.shape
    depth = p.shape[0]
    assert S % ROW_TILE == 0 and ROW_TILE % T == 0
    moba_bias, swa_bias = _bias_tables(rel_bias)
    for i in range(depth):
        w_nn, w_t = _split_in_weights(w_in[i])
        k_a, g_a, k_b, g_b, qt_a, vt_a, qt_b, vt_b = _inproj(x, norm_in[i][None, :], w_nn, w_t)
        a_out = _swa(sinks[i], qt_a, k_a, vt_a, swa_bias)
        b_out = _moba(qt_b, k_b, vt_b, moba_bias)
        x = _outproj(x, a_out, b_out, g_a, g_b, p[i],
                     w_out[i].astype(BF16), ple_norm[i][None, :], w_ple_gate[i].astype(BF16),
                     w_ple_proj[i].astype(BF16), final_norm[None, :], final=(i == depth - 1))
    return x
```

```python
import functools
import math

import jax
import jax.numpy as jnp
from jax import lax
from jax.experimental import pallas as pl
from jax.experimental.pallas import tpu as pltpu

F32 = jnp.float32
BF16 = jnp.bfloat16

HEAD_DIM = 64
SWA_Q_HEADS = 8
SWA_KV_HEADS = 2
MOBA_HEADS = 8
WINDOW = 128
MOBA_BLOCK = 256
MOBA_TOPK = 3
NUM_BUCKETS = 32
MAX_DISTANCE = 1024
RMS_EPS = 1e-6

SWA_WIDTH = SWA_Q_HEADS * HEAD_DIM
SWA_KV_WIDTH = SWA_KV_HEADS * HEAD_DIM
MOBA_WIDTH = MOBA_HEADS * HEAD_DIM
SWA_GROUP = SWA_Q_HEADS // SWA_KV_HEADS

T = MOBA_BLOCK
AUG = 128
V_ROWS = HEAD_DIM + 16
NEG = -1e30
LOG2E = math.log2(math.e)
N_NEAR = -(-(MAX_DISTANCE + T) // T)
KEY_GROUP = 8
QUERY_GROUP = 4
ITEM_UNROLL = 8
assert WINDOW <= T // 2
ROW_TILE = 512
VMEM_LIMIT = 56 * 1024 * 1024


def _rel_bucket(dist):
    n = jnp.maximum(dist, 0)
    max_exact = NUM_BUCKETS // 2
    nf = jnp.maximum(n, 1).astype(F32)
    large = max_exact + (jnp.log(nf / max_exact) / math.log(MAX_DISTANCE / max_exact)
                         * (NUM_BUCKETS - max_exact)).astype(jnp.int32)
    large = jnp.minimum(large, NUM_BUCKETS - 1)
    return jnp.where(n < max_exact, n, large)


def _bucket_span(dist_lo, dist_hi):
    def bucket(n):
        max_exact = NUM_BUCKETS // 2
        if n < max_exact:
            return n
        scaled = math.log(n / max_exact) / math.log(MAX_DISTANCE / max_exact) * (NUM_BUCKETS - max_exact)
        return min(max_exact + int(scaled), NUM_BUCKETS - 1)

    return max(bucket(dist_lo) - 1, 0), min(bucket(dist_hi) + 1, NUM_BUCKETS - 1) + 1


def _bias_kernel(rel_ref, moba_ref, swa_ref):
    h = pl.program_id(0)
    kk = lax.broadcasted_iota(jnp.int32, (T, T), 0)
    qq = lax.broadcasted_iota(jnp.int32, (T, T), 1)
    rel = qq - kk

    def lookup(bucket, col, dist_lo, dist_hi):
        acc = jnp.zeros((T, T), F32)
        for b in range(*_bucket_span(dist_lo, dist_hi)):
            acc = jnp.where(bucket == b, rel_ref[b, col] * LOG2E, acc)
        return acc

    for d in range(N_NEAR):
        dist = rel + d * T
        bias = lookup(_rel_bucket(dist), SWA_Q_HEADS + h, max(d * T - (T - 1), 0), d * T + (T - 1))
        if d == 0:
            bias = jnp.where(dist >= 0, bias, NEG)
        moba_ref[0, d] = bias
    far = jnp.full((T, T), N_NEAR * T, jnp.int32)
    moba_ref[0, N_NEAR] = lookup(_rel_bucket(far), SWA_Q_HEADS + h, N_NEAR * T, N_NEAR * T)

    for d in range(2):
        dist = rel + d * T
        bias = lookup(_rel_bucket(dist), h, 0, WINDOW - 1)
        valid = (dist >= 0) & (dist < WINDOW)
        swa_ref[0, d] = jnp.where(valid, bias, NEG)


def _bias_tables(rel_bias):
    return pl.pallas_call(
        _bias_kernel,
        grid=(MOBA_HEADS,),
        in_specs=[pl.BlockSpec(memory_space=pltpu.SMEM)],
        out_specs=[
            pl.BlockSpec((1, N_NEAR + 1, T, T), lambda h: (h, 0, 0, 0)),
            pl.BlockSpec((1, 2, T, T), lambda h: (h, 0, 0, 0)),
        ],
        out_shape=[
            jax.ShapeDtypeStruct((MOBA_HEADS, N_NEAR + 1, T, T), F32),
            jax.ShapeDtypeStruct((SWA_Q_HEADS, 2, T, T), F32),
        ],
        name="bias_tables",
    )(rel_bias)


def _inproj_kernel(x_ref, g_ref, wnn_ref, wt_ref,
                   ka_ref, ga_ref, kb_ref, gb_ref,
                   qta_ref, vta_ref, qtb_ref, vtb_ref):
    x = x_ref[0]
    r = lax.rsqrt(jnp.mean(x * x, axis=-1, keepdims=True) + RMS_EPS)
    h = ((x * r) * g_ref[...]).astype(BF16)

    tok = jnp.dot(h, wnn_ref[...], preferred_element_type=F32)
    o = 0
    for ref, width in ((ka_ref, SWA_KV_WIDTH), (ga_ref, SWA_WIDTH),
                       (kb_ref, MOBA_WIDTH), (gb_ref, MOBA_WIDTH)):
        ref[0] = tok[:, o:o + width].astype(BF16)
        o += width

    feat = lax.dot_general(wt_ref[...], h, (((1,), (1,)), ((), ())),
                           preferred_element_type=F32)
    o = 0
    for ref, width in ((qta_ref, SWA_WIDTH), (vta_ref, SWA_KV_WIDTH),
                       (qtb_ref, MOBA_WIDTH), (vtb_ref, MOBA_WIDTH)):
        for c in range(ROW_TILE // T):
            ref[0, c] = feat[o:o + width, c * T:(c + 1) * T].astype(BF16)
        o += width


def _inproj(x, gain, w_nn, w_t):
    B, S, D = x.shape
    nt = S // ROW_TILE
    cpt = ROW_TILE // T
    n_nn = w_nn.shape[1]
    n_t = w_t.shape[0]

    def tok_spec(width):
        return pl.BlockSpec((1, ROW_TILE, width), lambda b, t: (b, t, 0))

    def feat_spec(width):
        return pl.BlockSpec((1, cpt, width, T), lambda b, t: (b, t, 0, 0))

    def tok_shape(width):
        return jax.ShapeDtypeStruct((B, S, width), BF16)

    def feat_shape(width):
        return jax.ShapeDtypeStruct((B, S // T, width, T), BF16)

    return pl.pallas_call(
        _inproj_kernel,
        grid=(B, nt),
        in_specs=[
            pl.BlockSpec((1, ROW_TILE, D), lambda b, t: (b, t, 0)),
            pl.BlockSpec((1, D), lambda b, t: (0, 0)),
            pl.BlockSpec((D, n_nn), lambda b, t: (0, 0)),
            pl.BlockSpec((n_t, D), lambda b, t: (0, 0)),
        ],
        out_specs=[tok_spec(SWA_KV_WIDTH), tok_spec(SWA_WIDTH), tok_spec(MOBA_WIDTH), tok_spec(MOBA_WIDTH),
                   feat_spec(SWA_WIDTH), feat_spec(SWA_KV_WIDTH), feat_spec(MOBA_WIDTH), feat_spec(MOBA_WIDTH)],
        out_shape=[tok_shape(SWA_KV_WIDTH), tok_shape(SWA_WIDTH), tok_shape(MOBA_WIDTH), tok_shape(MOBA_WIDTH),
                   feat_shape(SWA_WIDTH), feat_shape(SWA_KV_WIDTH), feat_shape(MOBA_WIDTH), feat_shape(MOBA_WIDTH)],
        compiler_params=pltpu.CompilerParams(
            dimension_semantics=("arbitrary", "arbitrary"), vmem_limit_bytes=VMEM_LIMIT),
        name="norm_inproj",
    )(x, gain, w_nn, w_t)


def _ones_pad_rows(width):
    row = lax.broadcasted_iota(jnp.int32, (V_ROWS - HEAD_DIM, width), 0)
    return jnp.where(row == 0, 1.0, 0.0).astype(BF16)


def _swa_kernel(sink_ref, qt_ref, kcur_ref, kprev_ref, vcur_ref, vprev_ref, bias_ref,
                o_ref, ot_ref, sc_ref, sp_ref):
    blk = pl.program_id(1)
    half = T // 2
    k_cur = kcur_ref[0]
    k_prev = kprev_ref[0, half:, :]
    zeros_q = jnp.zeros((HEAD_DIM, T), BF16)
    prev_pen = jnp.where(blk == 0, NEG, 0.0).astype(F32)

    maxes = []
    for head in range(SWA_Q_HEADS):
        g = head // SWA_GROUP
        qt = qt_ref[0, 0, head * HEAD_DIM:(head + 1) * HEAD_DIM, :]
        parts = [zeros_q] * SWA_KV_HEADS
        parts[g] = qt
        q_pad = jnp.concatenate(parts, axis=0)
        s_cur = jnp.dot(k_cur, q_pad, preferred_element_type=F32) + bias_ref[head, 0]
        s_prev = jnp.dot(k_prev, q_pad, preferred_element_type=F32) + (bias_ref[head, 1, half:, :] + prev_pen)
        sc_ref[head] = s_cur
        sp_ref[head] = s_prev
        m = jnp.maximum(jnp.max(s_cur, axis=0, keepdims=True), jnp.max(s_prev, axis=0, keepdims=True))
        maxes.append(jnp.maximum(m, sink_ref[head] * LOG2E))

    for head in range(SWA_Q_HEADS):
        g = head // SWA_GROUP
        rows = slice(g * HEAD_DIM, (g + 1) * HEAD_DIM)
        v_cur = jnp.concatenate([vcur_ref[0, 0, rows, :], _ones_pad_rows(T)], axis=0)
        v_prev = jnp.concatenate([vprev_ref[0, 0, rows, half:], _ones_pad_rows(half)], axis=0)
        m = maxes[head]
        p_cur = jnp.exp2(sc_ref[head] - m).astype(BF16)
        p_prev = jnp.exp2(sp_ref[head] - m).astype(BF16)
        acc = (jnp.dot(v_cur, p_cur, preferred_element_type=F32)
               + jnp.dot(v_prev, p_prev, preferred_element_type=F32))
        den = acc[HEAD_DIM:HEAD_DIM + 1] + jnp.exp2(sink_ref[head] * LOG2E - m)
        ot_ref[head * HEAD_DIM:(head + 1) * HEAD_DIM, :] = acc[:HEAD_DIM] / den
    o_ref[0] = ot_ref[...].T.astype(BF16)


def _swa(sinks, qt_a, k_a, vt_a, swa_bias):
    B, nblk = qt_a.shape[0], qt_a.shape[1]
    S = nblk * T
    prev = lambda b, i: (b, jnp.maximum(i - 1, 0), 0)
    prev4 = lambda b, i: (b, jnp.maximum(i - 1, 0), 0, 0)
    return pl.pallas_call(
        _swa_kernel,
        grid=(B, nblk),
        in_specs=[
            pl.BlockSpec(memory_space=pltpu.SMEM),
            pl.BlockSpec((1, 1, SWA_WIDTH, T), lambda b, i: (b, i, 0, 0)),
            pl.BlockSpec((1, T, SWA_KV_WIDTH), lambda b, i: (b, i, 0)),
            pl.BlockSpec((1, T, SWA_KV_WIDTH), prev),
            pl.BlockSpec((1, 1, SWA_KV_WIDTH, T), lambda b, i: (b, i, 0, 0)),
            pl.BlockSpec((1, 1, SWA_KV_WIDTH, T), prev4),
            pl.BlockSpec((SWA_Q_HEADS, 2, T, T), lambda b, i: (0, 0, 0, 0)),
        ],
        out_specs=pl.BlockSpec((1, T, SWA_WIDTH), lambda b, i: (b, i, 0)),
        out_shape=jax.ShapeDtypeStruct((B, S, SWA_WIDTH), BF16),
        scratch_shapes=[pltpu.VMEM((SWA_WIDTH, T), F32),
                        pltpu.VMEM((SWA_Q_HEADS, T, T), F32),
                        pltpu.VMEM((SWA_Q_HEADS, T // 2, T), F32)],
        compiler_params=pltpu.CompilerParams(
            dimension_semantics=("arbitrary", "arbitrary"), vmem_limit_bytes=VMEM_LIMIT),
        name="swa_attention",
    )(sinks, qt_a, k_a, k_a, vt_a, vt_a, swa_bias)


def _moba_kernel(qt_ref, k_ref, vt_ref, bias_ref, o_ref,
                 kaug_ref, vaug_ref, kmean_ref, qaug_ref, acc_ref,
                 s00_ref, s01_ref, s10_ref, s11_ref, *, nblk):
    s_refs = ((s00_ref, s01_ref), (s10_ref, s11_ref))
    heads = range(2)
    head_rows = [slice(hl * HEAD_DIM, (hl + 1) * HEAD_DIM) for hl in heads]

    rows = KEY_GROUP * T
    sel_r = lax.broadcasted_iota(jnp.int32, (2 * HEAD_DIM, AUG), 0)
    sel_c = lax.broadcasted_iota(jnp.int32, (2 * HEAD_DIM, AUG), 1)
    picks = [jnp.where((sel_r == sel_c + hl * HEAD_DIM) & (sel_c < HEAD_DIM), 1.0, 0.0).astype(BF16)
             for hl in heads]
    lane = lax.broadcasted_iota(jnp.int32, (rows, AUG), 1)
    row_blk = lax.shift_right_logical(lax.broadcasted_iota(jnp.int32, (rows, AUG), 0), T.bit_length() - 1)
    pad = _ones_pad_rows(T)

    def prep(c, carry):
        row0 = pl.multiple_of(c * rows, rows)
        blk0 = pl.multiple_of(c * KEY_GROUP, KEY_GROUP)
        kb = k_ref[0, pl.ds(row0, rows), :]
        onehot = jnp.where(lane == HEAD_DIM + blk0 + row_blk, 1.0, 0.0)
        for hl in heads:
            kp = jnp.dot(kb, picks[hl], preferred_element_type=F32)
            kaug_ref[hl, pl.ds(row0, rows), :] = (kp + onehot).astype(BF16)
            kmean_ref[hl, pl.ds(blk0, KEY_GROUP), :] = (
                jnp.sum(kp.reshape(KEY_GROUP, T, AUG), axis=1) * (1.0 / T))
            for r in range(KEY_GROUP):
                vaug_ref[hl, c * (KEY_GROUP // 2) + r // 2, :, (r % 2) * T:(r % 2 + 1) * T] = (
                    jnp.concatenate([vt_ref[0, blk0 + r, head_rows[hl], :], pad], axis=0))
        return carry

    lax.fori_loop(0, nblk // KEY_GROUP, prep, 0)

    width = QUERY_GROUP * T
    blk_id = lax.broadcasted_iota(jnp.int32, (nblk, width), 0)
    lane_blk = lax.shift_right_logical(lax.broadcasted_iota(jnp.int32, (nblk, width), 1), T.bit_length() - 1)
    zeros_q = jnp.zeros((AUG - HEAD_DIM, T), BF16)
    zeros_tail = jnp.zeros((AUG - HEAD_DIM - nblk, T), BF16)
    kmeans = []
    for hl in heads:
        km = kmean_ref[hl]
        km_hi = km.astype(BF16)
        kmeans.append((km_hi, (km - km_hi.astype(F32)).astype(BF16)))

    def route(c, carry):
        blk0 = pl.multiple_of(c * QUERY_GROUP, QUERY_GROUP)
        q_blk = blk0 + lane_blk
        for hl in heads:
            km_hi, km_lo = kmeans[hl]
            qts = [qt_ref[0, blk0 + x, head_rows[hl], :] for x in range(QUERY_GROUP)]
            scs = []
            for qt in qts:
                qs = jnp.concatenate([qt, zeros_q], axis=0)
                scs.append(jnp.dot(km_hi, qs, preferred_element_type=F32)
                           + jnp.dot(km_lo, qs, preferred_element_type=F32))
            sc = jnp.concatenate(scs, axis=1)
            sc = jnp.where(blk_id < q_blk, sc, -jnp.inf)
            chosen = blk_id == q_blk
            for _ in range(MOBA_TOPK):
                mx = jnp.max(sc, axis=0, keepdims=True)
                first = jnp.min(jnp.where(sc == mx, blk_id, nblk), axis=0, keepdims=True)
                hit = (blk_id == first) & (mx > -jnp.inf)
                chosen = chosen | hit
                sc = jnp.where(hit, -jnp.inf, sc)
            sel_bias = jnp.where(chosen, 0.0, NEG).astype(BF16)
            for x in range(QUERY_GROUP):
                qaug_ref[hl, blk0 + x] = jnp.concatenate(
                    [qts[x], sel_bias[:, x * T:(x + 1) * T], zeros_tail], axis=0)
        return carry

    lax.fori_loop(0, nblk // QUERY_GROUP, route, 0)

    def next_item(i, u):
        last = u + 1 == (i + 2) // 2
        return jnp.where(last, i + 1, i), jnp.where(last, 0, u + 1)

    def score(hl, slot, i, u, m):
        i = jnp.minimum(i, nblk - 1)
        row0 = pl.multiple_of(u * 2 * T, 2 * T)
        st = jnp.dot(kaug_ref[hl, pl.ds(row0, 2 * T), :], qaug_ref[hl, i], preferred_element_type=F32)
        s_even = st[0:T] + bias_ref[hl, jnp.clip(i - 2 * u, 0, N_NEAR)]
        s_odd = st[T:] + bias_ref[hl, jnp.clip(i - 2 * u - 1, 0, N_NEAR)]
        s_refs[hl][slot][0:T, :] = s_even
        s_refs[hl][slot][T:, :] = s_odd
        m = jnp.where(u == 0, NEG, m)
        top = jnp.maximum(jnp.max(s_even, axis=0, keepdims=True), jnp.max(s_odd, axis=0, keepdims=True))
        m_new = jnp.maximum(m, top)
        return m_new, jnp.exp2(m - m_new)

    def accumulate(hl, slot, i, u, m, alpha, acc):
        p = jnp.exp2(s_refs[hl][slot][...] - m).astype(BF16)
        acc = acc * alpha + jnp.dot(vaug_ref[hl, u], p, preferred_element_type=F32)
        acc_ref[hl, i] = acc
        return acc

    def item_step(slot, i, u, state):
        i_nxt, u_nxt = next_item(i, u)
        nxt = [score(hl, 1 - slot, i_nxt, u_nxt, state[hl][0]) for hl in heads]
        state = tuple(nxt[hl] + (accumulate(hl, slot, i, u, *state[hl]),) for hl in heads)
        return i_nxt, u_nxt, state

    def item_group(_, carry):
        i, u, state = carry
        for n in range(ITEM_UNROLL):
            i, u, state = item_step(n % 2, i, u, state)
        return i, u, state

    zero = jnp.int32(0)
    m0 = jnp.full((1, T), NEG, F32)
    state = tuple(score(hl, 0, zero, zero, m0) + (jnp.zeros((V_ROWS, T), F32),) for hl in heads)
    n_items = sum((i + 2) // 2 for i in range(nblk))
    assert n_items % ITEM_UNROLL == 0 and ITEM_UNROLL % 2 == 0
    lax.fori_loop(0, n_items // ITEM_UNROLL, item_group, (zero, zero, state))

    def finish(c, carry):
        for x in range(QUERY_GROUP):
            i = c * QUERY_GROUP + x
            outs = []
            for hl in heads:
                acc = acc_ref[hl, i]
                outs.append(acc[:HEAD_DIM] / acc[HEAD_DIM:HEAD_DIM + 1])
            o_ref[0, pl.ds(pl.multiple_of(i * T, T), T), :] = jnp.concatenate(outs, axis=0).T.astype(BF16)
        return carry

    lax.fori_loop(0, nblk // QUERY_GROUP, finish, 0)


def _moba(qt_b, k_b, vt_b, moba_bias):
    B, nblk = qt_b.shape[0], qt_b.shape[1]
    S = nblk * T
    assert HEAD_DIM + nblk <= AUG, "block one-hot must fit the augmented contraction width"
    assert nblk % KEY_GROUP == 0 and nblk % QUERY_GROUP == 0 and nblk % 2 == 0
    pair = 2 * HEAD_DIM
    return pl.pallas_call(
        functools.partial(_moba_kernel, nblk=nblk),
        grid=(B, MOBA_HEADS // 2),
        in_specs=[
            pl.BlockSpec((1, nblk, pair, T), lambda b, h: (b, 0, h, 0)),
            pl.BlockSpec((1, S, pair), lambda b, h: (b, 0, h)),
            pl.BlockSpec((1, nblk, pair, T), lambda b, h: (b, 0, h, 0)),
            pl.BlockSpec((2, N_NEAR + 1, T, T), lambda b, h: (h, 0, 0, 0)),
        ],
        out_specs=pl.BlockSpec((1, S, pair), lambda b, h: (b, 0, h)),
        out_shape=jax.ShapeDtypeStruct((B, S, MOBA_WIDTH), BF16),
        scratch_shapes=[
            pltpu.VMEM((2, S, AUG), BF16),
            pltpu.VMEM((2, nblk // 2, V_ROWS, 2 * T), BF16),
            pltpu.VMEM((2, nblk, AUG), F32),
            pltpu.VMEM((2, nblk, AUG, T), BF16),
            pltpu.VMEM((2, nblk, V_ROWS, T), F32),
        ] + [pltpu.VMEM((2 * T, T), F32)] * 4,
        compiler_params=pltpu.CompilerParams(
            dimension_semantics=("arbitrary", "arbitrary"), vmem_limit_bytes=VMEM_LIMIT),
        name="moba_attention",
    )(qt_b, k_b, vt_b, moba_bias)


def _sigmoid(v):
    return 1.0 / (1.0 + jnp.exp(-v))


def _rms(v, gain):
    r = lax.rsqrt(jnp.mean(v * v, axis=-1, keepdims=True) + RMS_EPS)
    return (v * r) * gain


def _outproj_kernel(x_ref, ao_ref, bo_ref, ga_ref, gb_ref, p_ref,
                    wo_ref, pg_ref, wg_ref, wp_ref, fg_ref, o_ref, *, final):
    ga = ga_ref[0].astype(F32)
    gb = gb_ref[0].astype(F32)
    ma = (ao_ref[0].astype(F32) * (ga * _sigmoid(ga))).astype(BF16)
    mb = (bo_ref[0].astype(F32) * (gb * _sigmoid(gb))).astype(BF16)
    x1 = (x_ref[0]
          + jnp.dot(ma, wo_ref[0:SWA_WIDTH, :], preferred_element_type=F32)
          + jnp.dot(mb, wo_ref[SWA_WIDTH:, :], preferred_element_type=F32))
    n = _rms(x1, pg_ref[...]).astype(BF16)
    gate = _sigmoid(jnp.dot(n, wg_ref[...], preferred_element_type=F32))
    up = jnp.dot(p_ref[0].astype(BF16), wp_ref[...], preferred_element_type=F32)
    y = x1 + up * gate
    if final:
        y = _rms(y, fg_ref[...])
    o_ref[0] = y


def _outproj(x, a_out, b_out, g_a, g_b, p, w_out, ple_gain, w_gate, w_proj, final_gain, final):
    B, S, D = x.shape
    nt = S // ROW_TILE
    ple = p.shape[-1]

    def tok(width):
        return pl.BlockSpec((1, ROW_TILE, width), lambda b, t: (b, t, 0))

    def whole(shape):
        return pl.BlockSpec(shape, lambda b, t: (0,) * len(shape))

    return pl.pallas_call(
        functools.partial(_outproj_kernel, final=final),
        grid=(B, nt),
        in_specs=[tok(D), tok(SWA_WIDTH), tok(MOBA_WIDTH), tok(SWA_WIDTH), tok(MOBA_WIDTH), tok(ple),
                  whole((D, D)), whole((1, D)), whole((D, D)), whole((ple, D)), whole((1, D))],
        out_specs=tok(D),
        out_shape=jax.ShapeDtypeStruct((B, S, D), F32),
        compiler_params=pltpu.CompilerParams(
            dimension_semantics=("arbitrary", "arbitrary"), vmem_limit_bytes=VMEM_LIMIT),
        name="outproj_ple",
    )(x, a_out, b_out, g_a, g_b, p, w_out, ple_gain, w_gate, w_proj, final_gain)


def _split_in_weights(w):
    widths = (SWA_WIDTH, SWA_KV_WIDTH, SWA_KV_WIDTH, SWA_WIDTH,
              MOBA_WIDTH, MOBA_WIDTH, MOBA_WIDTH, MOBA_WIDTH)
    offs = [0]
    for wd in widths:
        offs.append(offs[-1] + wd)
    a_q, a_k, a_v, a_g, b_q, b_k, b_v, b_g = (w[:, offs[n]:offs[n + 1]] for n in range(8))
    scale = HEAD_DIM ** -0.5 * LOG2E
    w_nn = jnp.concatenate([a_k, a_g, b_k, b_g], axis=1).astype(BF16)
    w_t = jnp.concatenate([a_q * scale, a_v, b_q * scale, b_v], axis=1).T.astype(BF16)
    return w_nn, w_t


def kernel(x, p, norm_in, w_in, sinks, rel_bias, w_out, ple_norm, w_ple_gate, w_ple_proj, final_norm):
    B, S, D = x.shape
    depth = p.shape[0]
    assert S % ROW_TILE == 0 and ROW_TILE % T == 0
    moba_bias, swa_bias = _bias_tables(rel_bias)
    for i in range(depth):
        w_nn, w_t = _split_in_weights(w_in[i])
        k_a, g_a, k_b, g_b, qt_a, vt_a, qt_b, vt_b = _inproj(x, norm_in[i][None, :], w_nn, w_t)
        a_out = _swa(sinks[i], qt_a, k_a, vt_a, swa_bias)
        b_out = _moba(qt_b, k_b, vt_b, moba_bias)
        x = _outproj(x, a_out, b_out, g_a, g_b, p[i],
                     w_out[i].astype(BF16), ple_norm[i][None, :], w_ple_gate[i].astype(BF16),
                     w_ple_proj[i].astype(BF16), final_norm[None, :], final=(i == depth - 1))
    return x
```

```python
import functools
import math

import jax
import jax.numpy as jnp
from jax import lax
from jax.experimental import pallas as pl
from jax.experimental.pallas import tpu as pltpu

F32 = jnp.float32
BF16 = jnp.bfloat16

HEAD_DIM = 64
SWA_Q_HEADS = 8
SWA_KV_HEADS = 2
MOBA_HEADS = 8
WINDOW = 128
MOBA_BLOCK = 256
MOBA_TOPK = 3
NUM_BUCKETS = 32
MAX_DISTANCE = 1024
RMS_EPS = 1e-6

SWA_WIDTH = SWA_Q_HEADS * HEAD_DIM
SWA_KV_WIDTH = SWA_KV_HEADS * HEAD_DIM
MOBA_WIDTH = MOBA_HEADS * HEAD_DIM
SWA_GROUP = SWA_Q_HEADS // SWA_KV_HEADS

T = MOBA_BLOCK
AUG = 128
V_ROWS = HEAD_DIM + 16
NEG = -1e30
LOG2E = math.log2(math.e)
N_NEAR = -(-(MAX_DISTANCE + T) // T)
KEY_GROUP = 8
QUERY_GROUP = 4
ITEM_UNROLL = 16
assert WINDOW <= T // 2
ROW_TILE = 512
OUT_CHUNKS = 2
VMEM_LIMIT = 56 * 1024 * 1024


def _rel_bucket(dist):
    n = jnp.maximum(dist, 0)
    max_exact = NUM_BUCKETS // 2
    nf = jnp.maximum(n, 1).astype(F32)
    large = max_exact + (jnp.log(nf / max_exact) / math.log(MAX_DISTANCE / max_exact)
                         * (NUM_BUCKETS - max_exact)).astype(jnp.int32)
    large = jnp.minimum(large, NUM_BUCKETS - 1)
    return jnp.where(n < max_exact, n, large)


def _bucket_span(dist_lo, dist_hi):
    def bucket(n):
        max_exact = NUM_BUCKETS // 2
        if n < max_exact:
            return n
        scaled = math.log(n / max_exact) / math.log(MAX_DISTANCE / max_exact) * (NUM_BUCKETS - max_exact)
        return min(max_exact + int(scaled), NUM_BUCKETS - 1)

    return max(bucket(dist_lo) - 1, 0), min(bucket(dist_hi) + 1, NUM_BUCKETS - 1) + 1


def _bias_kernel(rel_ref, moba_ref, swa_ref):
    h = pl.program_id(0)
    kk = lax.broadcasted_iota(jnp.int32, (T, T), 0)
    qq = lax.broadcasted_iota(jnp.int32, (T, T), 1)
    rel = qq - kk

    def lookup(bucket, col, dist_lo, dist_hi):
        acc = jnp.zeros((T, T), F32)
        for b in range(*_bucket_span(dist_lo, dist_hi)):
            acc = jnp.where(bucket == b, rel_ref[b, col] * LOG2E, acc)
        return acc

    for d in range(N_NEAR):
        dist = rel + d * T
        bias = lookup(_rel_bucket(dist), SWA_Q_HEADS + h, max(d * T - (T - 1), 0), d * T + (T - 1))
        if d == 0:
            bias = jnp.where(dist >= 0, bias, NEG)
        moba_ref[0, d] = bias
    far = jnp.full((T, T), N_NEAR * T, jnp.int32)
    moba_ref[0, N_NEAR] = lookup(_rel_bucket(far), SWA_Q_HEADS + h, N_NEAR * T, N_NEAR * T)

    for d in range(2):
        dist = rel + d * T
        bias = lookup(_rel_bucket(dist), h, 0, WINDOW - 1)
        valid = (dist >= 0) & (dist < WINDOW)
        swa_ref[0, d] = jnp.where(valid, bias, NEG)


def _bias_tables(rel_bias):
    return pl.pallas_call(
        _bias_kernel,
        grid=(MOBA_HEADS,),
        in_specs=[pl.BlockSpec(memory_space=pltpu.SMEM)],
        out_specs=[
            pl.BlockSpec((1, N_NEAR + 1, T, T), lambda h: (h, 0, 0, 0)),
            pl.BlockSpec((1, 2, T, T), lambda h: (h, 0, 0, 0)),
        ],
        out_shape=[
            jax.ShapeDtypeStruct((MOBA_HEADS, N_NEAR + 1, T, T), F32),
            jax.ShapeDtypeStruct((SWA_Q_HEADS, 2, T, T), F32),
        ],
        name="bias_tables",
    )(rel_bias)


def _inproj_kernel(x_ref, g_ref, wnn_ref, wt_ref,
                   ka_ref, ga_ref, kb_ref, gb_ref,
                   qta_ref, vta_ref, qtb_ref, vtb_ref):
    x = x_ref[0]
    r = lax.rsqrt(jnp.mean(x * x, axis=-1, keepdims=True) + RMS_EPS)
    h = ((x * r) * g_ref[...]).astype(BF16)

    tok = jnp.dot(h, wnn_ref[...], preferred_element_type=F32)
    o = 0
    for ref, width in ((ka_ref, SWA_KV_WIDTH), (ga_ref, SWA_WIDTH),
                       (kb_ref, MOBA_WIDTH), (gb_ref, MOBA_WIDTH)):
        ref[0] = tok[:, o:o + width].astype(BF16)
        o += width

    feat = lax.dot_general(wt_ref[...], h, (((1,), (1,)), ((), ())),
                           preferred_element_type=F32)
    o = 0
    for ref, width in ((qta_ref, SWA_WIDTH), (vta_ref, SWA_KV_WIDTH),
                       (qtb_ref, MOBA_WIDTH), (vtb_ref, MOBA_WIDTH)):
        for c in range(ROW_TILE // T):
            ref[0, c] = feat[o:o + width, c * T:(c + 1) * T].astype(BF16)
        o += width


def _inproj(x, gain, w_nn, w_t):
    B, S, D = x.shape
    nt = S // ROW_TILE
    cpt = ROW_TILE // T
    n_nn = w_nn.shape[1]
    n_t = w_t.shape[0]

    def tok_spec(width):
        return pl.BlockSpec((1, ROW_TILE, width), lambda b, t: (b, t, 0))

    def feat_spec(width):
        return pl.BlockSpec((1, cpt, width, T), lambda b, t: (b, t, 0, 0))

    def tok_shape(width):
        return jax.ShapeDtypeStruct((B, S, width), BF16)

    def feat_shape(width):
        return jax.ShapeDtypeStruct((B, S // T, width, T), BF16)

    return pl.pallas_call(
        _inproj_kernel,
        grid=(B, nt),
        in_specs=[
            pl.BlockSpec((1, ROW_TILE, D), lambda b, t: (b, t, 0)),
            pl.BlockSpec((1, D), lambda b, t: (0, 0)),
            pl.BlockSpec((D, n_nn), lambda b, t: (0, 0)),
            pl.BlockSpec((n_t, D), lambda b, t: (0, 0)),
        ],
        out_specs=[tok_spec(SWA_KV_WIDTH), tok_spec(SWA_WIDTH), tok_spec(MOBA_WIDTH), tok_spec(MOBA_WIDTH),
                   feat_spec(SWA_WIDTH), feat_spec(SWA_KV_WIDTH), feat_spec(MOBA_WIDTH), feat_spec(MOBA_WIDTH)],
        out_shape=[tok_shape(SWA_KV_WIDTH), tok_shape(SWA_WIDTH), tok_shape(MOBA_WIDTH), tok_shape(MOBA_WIDTH),
                   feat_shape(SWA_WIDTH), feat_shape(SWA_KV_WIDTH), feat_shape(MOBA_WIDTH), feat_shape(MOBA_WIDTH)],
        compiler_params=pltpu.CompilerParams(
            dimension_semantics=("arbitrary", "arbitrary"), vmem_limit_bytes=VMEM_LIMIT),
        name="norm_inproj",
    )(x, gain, w_nn, w_t)


def _ones_pad_rows(width):
    row = lax.broadcasted_iota(jnp.int32, (V_ROWS - HEAD_DIM, width), 0)
    return jnp.where(row == 0, 1.0, 0.0).astype(BF16)


def _swa_kernel(sink_ref, qt_ref, kcur_ref, kprev_ref, vcur_ref, vprev_ref, bias_ref,
                o_ref, ot_ref, sc_ref, sp_ref):
    blk = pl.program_id(1)
    half = T // 2
    k_cur = kcur_ref[0]
    k_prev = kprev_ref[0, half:, :]
    zeros_q = jnp.zeros((HEAD_DIM, T), BF16)
    prev_pen = jnp.where(blk == 0, NEG, 0.0).astype(F32)

    def score(head):
        g = head // SWA_GROUP
        qt = qt_ref[0, 0, head * HEAD_DIM:(head + 1) * HEAD_DIM, :]
        parts = [zeros_q] * SWA_KV_HEADS
        parts[g] = qt
        q_pad = jnp.concatenate(parts, axis=0)
        s_cur = jnp.dot(k_cur, q_pad, preferred_element_type=F32) + bias_ref[head, 0]
        s_prev = jnp.dot(k_prev, q_pad, preferred_element_type=F32) + (bias_ref[head, 1, half:, :] + prev_pen)
        sc_ref[head] = s_cur
        sp_ref[head] = s_prev
        m = jnp.maximum(jnp.max(s_cur, axis=0, keepdims=True), jnp.max(s_prev, axis=0, keepdims=True))
        return jnp.maximum(m, sink_ref[head] * LOG2E)

    def finish(head, m):
        g = head // SWA_GROUP
        rows = slice(g * HEAD_DIM, (g + 1) * HEAD_DIM)
        v_cur = jnp.concatenate([vcur_ref[0, 0, rows, :], _ones_pad_rows(T)], axis=0)
        v_prev = jnp.concatenate([vprev_ref[0, 0, rows, half:], _ones_pad_rows(half)], axis=0)
        p_cur = jnp.exp2(sc_ref[head] - m).astype(BF16)
        p_prev = jnp.exp2(sp_ref[head] - m).astype(BF16)
        acc = (jnp.dot(v_cur, p_cur, preferred_element_type=F32)
               + jnp.dot(v_prev, p_prev, preferred_element_type=F32))
        den = acc[HEAD_DIM:HEAD_DIM + 1] + jnp.exp2(sink_ref[head] * LOG2E - m)
        ot_ref[head * HEAD_DIM:(head + 1) * HEAD_DIM, :] = acc[:HEAD_DIM] / den

    maxes = [score(head) for head in range(SWA_Q_HEADS)]
    for head in range(SWA_Q_HEADS):
        finish(head, maxes[head])
    o_ref[0] = ot_ref[...].T.astype(BF16)


def _swa(sinks, qt_a, k_a, vt_a, swa_bias):
    B, nblk = qt_a.shape[0], qt_a.shape[1]
    S = nblk * T
    prev = lambda b, i: (b, jnp.maximum(i - 1, 0), 0)
    prev4 = lambda b, i: (b, jnp.maximum(i - 1, 0), 0, 0)
    return pl.pallas_call(
        _swa_kernel,
        grid=(B, nblk),
        in_specs=[
            pl.BlockSpec(memory_space=pltpu.SMEM),
            pl.BlockSpec((1, 1, SWA_WIDTH, T), lambda b, i: (b, i, 0, 0)),
            pl.BlockSpec((1, T, SWA_KV_WIDTH), lambda b, i: (b, i, 0)),
            pl.BlockSpec((1, T, SWA_KV_WIDTH), prev),
            pl.BlockSpec((1, 1, SWA_KV_WIDTH, T), lambda b, i: (b, i, 0, 0)),
            pl.BlockSpec((1, 1, SWA_KV_WIDTH, T), prev4),
            pl.BlockSpec((SWA_Q_HEADS, 2, T, T), lambda b, i: (0, 0, 0, 0)),
        ],
        out_specs=pl.BlockSpec((1, T, SWA_WIDTH), lambda b, i: (b, i, 0)),
        out_shape=jax.ShapeDtypeStruct((B, S, SWA_WIDTH), BF16),
        scratch_shapes=[pltpu.VMEM((SWA_WIDTH, T), F32),
                        pltpu.VMEM((SWA_Q_HEADS, T, T), F32),
                        pltpu.VMEM((SWA_Q_HEADS, T // 2, T), F32)],
        compiler_params=pltpu.CompilerParams(
            dimension_semantics=("arbitrary", "arbitrary"), vmem_limit_bytes=VMEM_LIMIT),
        name="swa_attention",
    )(sinks, qt_a, k_a, k_a, vt_a, vt_a, swa_bias)


def _moba_kernel(qt_ref, k_ref, vt_ref, bias_ref, o_ref,
                 kaug_ref, vaug_ref, kmean_ref, qaug_ref, acc_ref,
                 s00_ref, s01_ref, s10_ref, s11_ref, *, nblk):
    s_refs = ((s00_ref, s01_ref), (s10_ref, s11_ref))
    heads = range(2)
    head_rows = [slice(hl * HEAD_DIM, (hl + 1) * HEAD_DIM) for hl in heads]

    rows = KEY_GROUP * T
    sel_r = lax.broadcasted_iota(jnp.int32, (2 * HEAD_DIM, AUG), 0)
    sel_c = lax.broadcasted_iota(jnp.int32, (2 * HEAD_DIM, AUG), 1)
    picks = [jnp.where((sel_r == sel_c + hl * HEAD_DIM) & (sel_c < HEAD_DIM), 1.0, 0.0).astype(BF16)
             for hl in heads]
    lane = lax.broadcasted_iota(jnp.int32, (rows, AUG), 1)
    row_blk = lax.shift_right_logical(lax.broadcasted_iota(jnp.int32, (rows, AUG), 0), T.bit_length() - 1)
    pad = _ones_pad_rows(T)

    def prep(c, carry):
        row0 = pl.multiple_of(c * rows, rows)
        blk0 = pl.multiple_of(c * KEY_GROUP, KEY_GROUP)
        kb = k_ref[0, pl.ds(row0, rows), :]
        onehot = jnp.where(lane == HEAD_DIM + blk0 + row_blk, 1.0, 0.0)
        for hl in heads:
            kp = jnp.dot(kb, picks[hl], preferred_element_type=F32)
            kaug_ref[hl, pl.ds(row0, rows), :] = (kp + onehot).astype(BF16)
            kmean_ref[hl, pl.ds(blk0, KEY_GROUP), :] = (
                jnp.sum(kp.reshape(KEY_GROUP, T, AUG), axis=1) * (1.0 / T))
            for r in range(KEY_GROUP):
                vaug_ref[hl, c * (KEY_GROUP // 2) + r // 2, :, (r % 2) * T:(r % 2 + 1) * T] = (
                    jnp.concatenate([vt_ref[0, blk0 + r, head_rows[hl], :], pad], axis=0))
        return carry

    lax.fori_loop(0, nblk // KEY_GROUP, prep, 0)

    width = QUERY_GROUP * T
    blk_id = lax.broadcasted_iota(jnp.int32, (nblk, width), 0)
    lane_blk = lax.shift_right_logical(lax.broadcasted_iota(jnp.int32, (nblk, width), 1), T.bit_length() - 1)
    zeros_q = jnp.zeros((AUG - HEAD_DIM, T), BF16)
    zeros_tail = jnp.zeros((AUG - HEAD_DIM - nblk, T), BF16)
    kmeans = []
    for hl in heads:
        km = kmean_ref[hl]
        km_hi = km.astype(BF16)
        kmeans.append((km_hi, (km - km_hi.astype(F32)).astype(BF16)))

    def route(c, carry):
        blk0 = pl.multiple_of(c * QUERY_GROUP, QUERY_GROUP)
        q_blk = blk0 + lane_blk
        for hl in heads:
            km_hi, km_lo = kmeans[hl]
            qts = [qt_ref[0, blk0 + x, head_rows[hl], :] for x in range(QUERY_GROUP)]
            scs = []
            for qt in qts:
                qs = jnp.concatenate([qt, zeros_q], axis=0)
                scs.append(jnp.dot(km_hi, qs, preferred_element_type=F32)
                           + jnp.dot(km_lo, qs, preferred_element_type=F32))
            sc = jnp.concatenate(scs, axis=1)
            sc = jnp.where(blk_id < q_blk, sc, -jnp.inf)
            chosen = blk_id == q_blk
            for _ in range(MOBA_TOPK):
                mx = jnp.max(sc, axis=0, keepdims=True)
                first = jnp.min(jnp.where(sc == mx, blk_id, nblk), axis=0, keepdims=True)
                hit = (blk_id == first) & (mx > -jnp.inf)
                chosen = chosen | hit
                sc = jnp.where(hit, -jnp.inf, sc)
            sel_bias = jnp.where(chosen, 0.0, NEG).astype(BF16)
            for x in range(QUERY_GROUP):
                qaug_ref[hl, blk0 + x] = jnp.concatenate(
                    [qts[x], sel_bias[:, x * T:(x + 1) * T], zeros_tail], axis=0)
        return carry

    lax.fori_loop(0, nblk // QUERY_GROUP, route, 0)

    def next_item(i, u):
        last = u + 1 == (i + 2) // 2
        return jnp.where(last, i + 1, i), jnp.where(last, 0, u + 1)

    def score(hl, slot, i, u, m):
        i = jnp.minimum(i, nblk - 1)
        row0 = pl.multiple_of(u * 2 * T, 2 * T)
        st = jnp.dot(kaug_ref[hl, pl.ds(row0, 2 * T), :], qaug_ref[hl, i], preferred_element_type=F32)
        s_even = st[0:T] + bias_ref[hl, jnp.clip(i - 2 * u, 0, N_NEAR)]
        s_odd = st[T:] + bias_ref[hl, jnp.clip(i - 2 * u - 1, 0, N_NEAR)]
        s_refs[hl][slot][0:T, :] = s_even
        s_refs[hl][slot][T:, :] = s_odd
        m = jnp.where(u == 0, NEG, m)
        top = jnp.maximum(jnp.max(s_even, axis=0, keepdims=True), jnp.max(s_odd, axis=0, keepdims=True))
        m_new = jnp.maximum(m, top)
        return m_new, jnp.exp2(m - m_new)

    def accumulate(hl, slot, i, u, m, alpha, acc):
        p = jnp.exp2(s_refs[hl][slot][...] - m).astype(BF16)
        acc = acc * alpha + jnp.dot(vaug_ref[hl, u], p, preferred_element_type=F32)
        acc_ref[hl, i] = acc
        return acc

    def item_step(slot, i, u, state):
        i_nxt, u_nxt = next_item(i, u)
        nxt = [score(hl, 1 - slot, i_nxt, u_nxt, state[hl][0]) for hl in heads]
        state = tuple(nxt[hl] + (accumulate(hl, slot, i, u, *state[hl]),) for hl in heads)
        return i_nxt, u_nxt, state

    def item_group(_, carry):
        i, u, state = carry
        for n in range(ITEM_UNROLL):
            i, u, state = item_step(n % 2, i, u, state)
        return i, u, state

    zero = jnp.int32(0)
    m0 = jnp.full((1, T), NEG, F32)
    state = tuple(score(hl, 0, zero, zero, m0) + (jnp.zeros((V_ROWS, T), F32),) for hl in heads)
    n_items = sum((i + 2) // 2 for i in range(nblk))
    assert n_items % ITEM_UNROLL == 0 and ITEM_UNROLL % 2 == 0
    lax.fori_loop(0, n_items // ITEM_UNROLL, item_group, (zero, zero, state))

    def finish(c, carry):
        for x in range(QUERY_GROUP):
            i = c * QUERY_GROUP + x
            outs = []
            for hl in heads:
                acc = acc_ref[hl, i]
                outs.append(acc[:HEAD_DIM] / acc[HEAD_DIM:HEAD_DIM + 1])
            o_ref[0, pl.ds(pl.multiple_of(i * T, T), T), :] = jnp.concatenate(outs, axis=0).T.astype(BF16)
        return carry

    lax.fori_loop(0, nblk // QUERY_GROUP, finish, 0)


def _moba(qt_b, k_b, vt_b, moba_bias):
    B, nblk = qt_b.shape[0], qt_b.shape[1]
    S = nblk * T
    assert HEAD_DIM + nblk <= AUG, "block one-hot must fit the augmented contraction width"
    assert nblk % KEY_GROUP == 0 and nblk % QUERY_GROUP == 0 and nblk % 2 == 0
    pair = 2 * HEAD_DIM
    return pl.pallas_call(
        functools.partial(_moba_kernel, nblk=nblk),
        grid=(B, MOBA_HEADS // 2),
        in_specs=[
            pl.BlockSpec((1, nblk, pair, T), lambda b, h: (b, 0, h, 0)),
            pl.BlockSpec((1, S, pair), lambda b, h: (b, 0, h)),
            pl.BlockSpec((1, nblk, pair, T), lambda b, h: (b, 0, h, 0)),
            pl.BlockSpec((2, N_NEAR + 1, T, T), lambda b, h: (h, 0, 0, 0)),
        ],
        out_specs=pl.BlockSpec((1, S, pair), lambda b, h: (b, 0, h)),
        out_shape=jax.ShapeDtypeStruct((B, S, MOBA_WIDTH), BF16),
        scratch_shapes=[
            pltpu.VMEM((2, S, AUG), BF16),
            pltpu.VMEM((2, nblk // 2, V_ROWS, 2 * T), BF16),
            pltpu.VMEM((2, nblk, AUG), F32),
            pltpu.VMEM((2, nblk, AUG, T), BF16),
            pltpu.VMEM((2, nblk, V_ROWS, T), F32),
        ] + [pltpu.VMEM((2 * T, T), F32)] * 4,
        compiler_params=pltpu.CompilerParams(
            dimension_semantics=("arbitrary", "arbitrary"), vmem_limit_bytes=VMEM_LIMIT),
        name="moba_attention",
    )(qt_b, k_b, vt_b, moba_bias)


def _sigmoid(v):
    return 1.0 / (1.0 + jnp.exp(-v))


def _rms(v, gain):
    r = lax.rsqrt(jnp.mean(v * v, axis=-1, keepdims=True) + RMS_EPS)
    return (v * r) * gain


def _outproj_kernel(x_ref, ao_ref, bo_ref, ga_ref, gb_ref, p_ref,
                    wo_ref, pg_ref, wg_ref, wp_ref, fg_ref, o_ref, *, final):
    chunk = ROW_TILE // OUT_CHUNKS
    chunks = [slice(c * chunk, (c + 1) * chunk) for c in range(OUT_CHUNKS)]

    def mix(rows):
        ga = ga_ref[0, rows, :].astype(F32)
        gb = gb_ref[0, rows, :].astype(F32)
        ma = (ao_ref[0, rows, :].astype(F32) * (ga * _sigmoid(ga))).astype(BF16)
        mb = (bo_ref[0, rows, :].astype(F32) * (gb * _sigmoid(gb))).astype(BF16)
        return (x_ref[0, rows, :]
                + jnp.dot(ma, wo_ref[0:SWA_WIDTH, :], preferred_element_type=F32)
                + jnp.dot(mb, wo_ref[SWA_WIDTH:, :], preferred_element_type=F32))

    def gate_up(rows, x1):
        n = _rms(x1, pg_ref[...]).astype(BF16)
        gate = _sigmoid(jnp.dot(n, wg_ref[...], preferred_element_type=F32))
        up = jnp.dot(p_ref[0, rows, :].astype(BF16), wp_ref[...], preferred_element_type=F32)
        return x1 + up * gate

    x1s = [mix(rows) for rows in chunks]
    ys = [gate_up(rows, x1) for rows, x1 in zip(chunks, x1s)]
    for rows, y in zip(chunks, ys):
        o_ref[0, rows, :] = _rms(y, fg_ref[...]) if final else y


def _outproj(x, a_out, b_out, g_a, g_b, p, w_out, ple_gain, w_gate, w_proj, final_gain, final):
    B, S, D = x.shape
    nt = S // ROW_TILE
    ple = p.shape[-1]

    def tok(width):
        return pl.BlockSpec((1, ROW_TILE, width), lambda b, t: (b, t, 0))

    def whole(shape):
        return pl.BlockSpec(shape, lambda b, t: (0,) * len(shape))

    return pl.pallas_call(
        functools.partial(_outproj_kernel, final=final),
        grid=(B, nt),
        in_specs=[tok(D), tok(SWA_WIDTH), tok(MOBA_WIDTH), tok(SWA_WIDTH), tok(MOBA_WIDTH), tok(ple),
                  whole((D, D)), whole((1, D)), whole((D, D)), whole((ple, D)), whole((1, D))],
        out_specs=tok(D),
        out_shape=jax.ShapeDtypeStruct((B, S, D), F32),
        compiler_params=pltpu.CompilerParams(
            dimension_semantics=("arbitrary", "arbitrary"), vmem_limit_bytes=VMEM_LIMIT),
        name="outproj_ple",
    )(x, a_out, b_out, g_a, g_b, p, w_out, ple_gain, w_gate, w_proj, final_gain)


def _split_in_weights(w):
    widths = (SWA_WIDTH, SWA_KV_WIDTH, SWA_KV_WIDTH, SWA_WIDTH,
              MOBA_WIDTH, MOBA_WIDTH, MOBA_WIDTH, MOBA_WIDTH)
    offs = [0]
    for wd in widths:
        offs.append(offs[-1] + wd)
    a_q, a_k, a_v, a_g, b_q, b_k, b_v, b_g = (w[:, offs[n]:offs[n + 1]] for n in range(8))
    scale = HEAD_DIM ** -0.5 * LOG2E
    w_nn = jnp.concatenate([a_k, a_g, b_k, b_g], axis=1).astype(BF16)
    w_t = jnp.concatenate([a_q * scale, a_v, b_q * scale, b_v], axis=1).T.astype(BF16)
    return w_nn, w_t


def kernel(x, p, norm_in, w_in, sinks, rel_bias, w_out, ple_norm, w_ple_gate, w_ple_proj, final_norm):
    B, S, D = x.shape
    depth = p.shape[0]
    assert S % ROW_TILE == 0 and ROW_TILE % T == 0
    moba_bias, swa_bias = _bias_tables(rel_bias)
    for i in range(depth):
        w_nn, w_t = _split_in_weights(w_in[i])
        k_a, g_a, k_b, g_b, qt_a, vt_a, qt_b, vt_b = _inproj(x, norm_in[i][None, :], w_nn, w_t)
        a_out = _swa(sinks[i], qt_a, k_a, vt_a, swa_bias)
        b_out = _moba(qt_b, k_b, vt_b, moba_bias)
        x = _outproj(x, a_out, b_out, g_a, g_b, p[i],
                     w_out[i].astype(BF16), ple_norm[i][None, :], w_ple_gate[i].astype(BF16),
                     w_ple_proj[i].astype(BF16), final_norm[None, :], final=(i == depth - 1))
    return x
```

```python
import functools
import math

import jax
import jax.numpy as jnp
from jax import lax
from jax.experimental import pallas as pl
from jax.experimental.pallas import tpu as pltpu

F32 = jnp.float32
BF16 = jnp.bfloat16

HEAD_DIM = 64
SWA_Q_HEADS = 8
SWA_KV_HEADS = 2
MOBA_HEADS = 8
WINDOW = 128
MOBA_BLOCK = 256
MOBA_TOPK = 3
NUM_BUCKETS = 32
MAX_DISTANCE = 1024
RMS_EPS = 1e-6

SWA_WIDTH = SWA_Q_HEADS * HEAD_DIM
SWA_KV_WIDTH = SWA_KV_HEADS * HEAD_DIM
MOBA_WIDTH = MOBA_HEADS * HEAD_DIM
SWA_GROUP = SWA_Q_HEADS // SWA_KV_HEADS

T = MOBA_BLOCK
AUG = 128
V_ROWS = HEAD_DIM + 16
NEG = -1e30
LOG2E = math.log2(math.e)
N_NEAR = -(-(MAX_DISTANCE + T) // T)
KEY_GROUP = 8
QUERY_GROUP = 4
ITEM_UNROLL = 16
assert WINDOW <= T // 2
ROW_TILE = 512
W_ROWS = 128
OUT_ROW_TILE = 1024
OUT_CHUNKS = 4
VMEM_LIMIT = 56 * 1024 * 1024


def _rel_bucket(dist):
    n = jnp.maximum(dist, 0)
    max_exact = NUM_BUCKETS // 2
    nf = jnp.maximum(n, 1).astype(F32)
    large = max_exact + (jnp.log(nf / max_exact) / math.log(MAX_DISTANCE / max_exact)
                         * (NUM_BUCKETS - max_exact)).astype(jnp.int32)
    large = jnp.minimum(large, NUM_BUCKETS - 1)
    return jnp.where(n < max_exact, n, large)


def _bucket_span(dist_lo, dist_hi):
    def bucket(n):
        max_exact = NUM_BUCKETS // 2
        if n < max_exact:
            return n
        scaled = math.log(n / max_exact) / math.log(MAX_DISTANCE / max_exact) * (NUM_BUCKETS - max_exact)
        return min(max_exact + int(scaled), NUM_BUCKETS - 1)

    return max(bucket(dist_lo) - 1, 0), min(bucket(dist_hi) + 1, NUM_BUCKETS - 1) + 1


def _bias_kernel(rel_ref, moba_ref, swa_ref):
    h = pl.program_id(0)
    kk = lax.broadcasted_iota(jnp.int32, (T, T), 0)
    qq = lax.broadcasted_iota(jnp.int32, (T, T), 1)
    rel = qq - kk

    def lookup(bucket, col, dist_lo, dist_hi):
        acc = jnp.zeros((T, T), F32)
        for b in range(*_bucket_span(dist_lo, dist_hi)):
            acc = jnp.where(bucket == b, rel_ref[b, col] * LOG2E, acc)
        return acc

    for d in range(N_NEAR):
        dist = rel + d * T
        bias = lookup(_rel_bucket(dist), SWA_Q_HEADS + h, max(d * T - (T - 1), 0), d * T + (T - 1))
        if d == 0:
            bias = jnp.where(dist >= 0, bias, NEG)
        moba_ref[0, d] = bias
    far = jnp.full((T, T), N_NEAR * T, jnp.int32)
    moba_ref[0, N_NEAR] = lookup(_rel_bucket(far), SWA_Q_HEADS + h, N_NEAR * T, N_NEAR * T)

    for d in range(2):
        dist = rel + d * T
        bias = lookup(_rel_bucket(dist), h, 0, WINDOW - 1)
        valid = (dist >= 0) & (dist < WINDOW)
        swa_ref[0, d] = jnp.where(valid, bias, NEG)


def _bias_tables(rel_bias):
    return pl.pallas_call(
        _bias_kernel,
        grid=(MOBA_HEADS,),
        in_specs=[pl.BlockSpec(memory_space=pltpu.SMEM)],
        out_specs=[
            pl.BlockSpec((1, N_NEAR + 1, T, T), lambda h: (h, 0, 0, 0)),
            pl.BlockSpec((1, 2, T, T), lambda h: (h, 0, 0, 0)),
        ],
        out_shape=[
            jax.ShapeDtypeStruct((MOBA_HEADS, N_NEAR + 1, T, T), F32),
            jax.ShapeDtypeStruct((SWA_Q_HEADS, 2, T, T), F32),
        ],
        name="bias_tables",
    )(rel_bias)


def _inproj_kernel(x_ref, g_ref, wnn_ref, wt_ref,
                   ka_ref, ga_ref, kb_ref, gb_ref,
                   qta_ref, vta_ref, qtb_ref, vtb_ref):
    x = x_ref[0]
    r = lax.rsqrt(jnp.mean(x * x, axis=-1, keepdims=True) + RMS_EPS)
    h = ((x * r) * g_ref[...]).astype(BF16)

    tok = jnp.dot(h, wnn_ref[...], preferred_element_type=F32)
    o = 0
    for ref, width in ((ka_ref, SWA_KV_WIDTH), (ga_ref, SWA_WIDTH),
                       (kb_ref, MOBA_WIDTH), (gb_ref, MOBA_WIDTH)):
        ref[0] = tok[:, o:o + width].astype(BF16)
        o += width

    feat = lax.dot_general(wt_ref[...], h, (((1,), (1,)), ((), ())),
                           preferred_element_type=F32)
    o = 0
    for ref, width in ((qta_ref, SWA_WIDTH), (vta_ref, SWA_KV_WIDTH),
                       (qtb_ref, MOBA_WIDTH), (vtb_ref, MOBA_WIDTH)):
        for c in range(ROW_TILE // T):
            ref[0, c] = feat[o:o + width, c * T:(c + 1) * T].astype(BF16)
        o += width


def _inproj(x, gain, w_nn, w_t):
    B, S, D = x.shape
    nt = S // ROW_TILE
    cpt = ROW_TILE // T
    n_nn = w_nn.shape[1]
    n_t = w_t.shape[0]

    def tok_spec(width):
        return pl.BlockSpec((1, ROW_TILE, width), lambda b, t: (b, t, 0))

    def feat_spec(width):
        return pl.BlockSpec((1, cpt, width, T), lambda b, t: (b, t, 0, 0))

    def tok_shape(width):
        return jax.ShapeDtypeStruct((B, S, width), BF16)

    def feat_shape(width):
        return jax.ShapeDtypeStruct((B, S // T, width, T), BF16)

    return pl.pallas_call(
        _inproj_kernel,
        grid=(B, nt),
        in_specs=[
            pl.BlockSpec((1, ROW_TILE, D), lambda b, t: (b, t, 0)),
            pl.BlockSpec((1, D), lambda b, t: (0, 0)),
            pl.BlockSpec((D, n_nn), lambda b, t: (0, 0)),
            pl.BlockSpec((n_t, D), lambda b, t: (0, 0)),
        ],
        out_specs=[tok_spec(SWA_KV_WIDTH), tok_spec(SWA_WIDTH), tok_spec(MOBA_WIDTH), tok_spec(MOBA_WIDTH),
                   feat_spec(SWA_WIDTH), feat_spec(SWA_KV_WIDTH), feat_spec(MOBA_WIDTH), feat_spec(MOBA_WIDTH)],
        out_shape=[tok_shape(SWA_KV_WIDTH), tok_shape(SWA_WIDTH), tok_shape(MOBA_WIDTH), tok_shape(MOBA_WIDTH),
                   feat_shape(SWA_WIDTH), feat_shape(SWA_KV_WIDTH), feat_shape(MOBA_WIDTH), feat_shape(MOBA_WIDTH)],
        compiler_params=pltpu.CompilerParams(
            dimension_semantics=("arbitrary", "arbitrary"), vmem_limit_bytes=VMEM_LIMIT),
        name="norm_inproj",
    )(x, gain, w_nn, w_t)


def _ones_pad_rows(width):
    row = lax.broadcasted_iota(jnp.int32, (V_ROWS - HEAD_DIM, width), 0)
    return jnp.where(row == 0, 1.0, 0.0).astype(BF16)


def _swa_kernel(sink_ref, qt_ref, kcur_ref, kprev_ref, vcur_ref, vprev_ref, bias_ref,
                o_ref, ot_ref, sc_ref, sp_ref):
    blk = pl.program_id(1)
    half = T // 2
    k_cur = kcur_ref[0]
    k_prev = kprev_ref[0, half:, :]
    zeros_q = jnp.zeros((HEAD_DIM, T), BF16)
    prev_pen = jnp.where(blk == 0, NEG, 0.0).astype(F32)

    def score(head):
        g = head // SWA_GROUP
        qt = qt_ref[0, 0, head * HEAD_DIM:(head + 1) * HEAD_DIM, :]
        parts = [zeros_q] * SWA_KV_HEADS
        parts[g] = qt
        q_pad = jnp.concatenate(parts, axis=0)
        s_cur = jnp.dot(k_cur, q_pad, preferred_element_type=F32) + bias_ref[head, 0]
        s_prev = jnp.dot(k_prev, q_pad, preferred_element_type=F32) + (bias_ref[head, 1, half:, :] + prev_pen)
        sc_ref[head] = s_cur
        sp_ref[head] = s_prev
        m = jnp.maximum(jnp.max(s_cur, axis=0, keepdims=True), jnp.max(s_prev, axis=0, keepdims=True))
        return jnp.maximum(m, sink_ref[head] * LOG2E)

    def finish(head, m):
        g = head // SWA_GROUP
        rows = slice(g * HEAD_DIM, (g + 1) * HEAD_DIM)
        v_cur = jnp.concatenate([vcur_ref[0, 0, rows, :], _ones_pad_rows(T)], axis=0)
        v_prev = jnp.concatenate([vprev_ref[0, 0, rows, half:], _ones_pad_rows(half)], axis=0)
        p_cur = jnp.exp2(sc_ref[head] - m).astype(BF16)
        p_prev = jnp.exp2(sp_ref[head] - m).astype(BF16)
        acc = (jnp.dot(v_cur, p_cur, preferred_element_type=F32)
               + jnp.dot(v_prev, p_prev, preferred_element_type=F32))
        den = acc[HEAD_DIM:HEAD_DIM + 1] + jnp.exp2(sink_ref[head] * LOG2E - m)
        ot_ref[head * HEAD_DIM:(head + 1) * HEAD_DIM, :] = acc[:HEAD_DIM] / den

    maxes = [score(head) for head in range(SWA_Q_HEADS)]
    for head in range(SWA_Q_HEADS):
        finish(head, maxes[head])
    o_ref[0] = ot_ref[...].T.astype(BF16)


def _swa(sinks, qt_a, k_a, vt_a, swa_bias):
    B, nblk = qt_a.shape[0], qt_a.shape[1]
    S = nblk * T
    prev = lambda b, i: (b, jnp.maximum(i - 1, 0), 0)
    prev4 = lambda b, i: (b, jnp.maximum(i - 1, 0), 0, 0)
    return pl.pallas_call(
        _swa_kernel,
        grid=(B, nblk),
        in_specs=[
            pl.BlockSpec(memory_space=pltpu.SMEM),
            pl.BlockSpec((1, 1, SWA_WIDTH, T), lambda b, i: (b, i, 0, 0)),
            pl.BlockSpec((1, T, SWA_KV_WIDTH), lambda b, i: (b, i, 0)),
            pl.BlockSpec((1, T, SWA_KV_WIDTH), prev),
            pl.BlockSpec((1, 1, SWA_KV_WIDTH, T), lambda b, i: (b, i, 0, 0)),
            pl.BlockSpec((1, 1, SWA_KV_WIDTH, T), prev4),
            pl.BlockSpec((SWA_Q_HEADS, 2, T, T), lambda b, i: (0, 0, 0, 0)),
        ],
        out_specs=pl.BlockSpec((1, T, SWA_WIDTH), lambda b, i: (b, i, 0)),
        out_shape=jax.ShapeDtypeStruct((B, S, SWA_WIDTH), BF16),
        scratch_shapes=[pltpu.VMEM((SWA_WIDTH, T), F32),
                        pltpu.VMEM((SWA_Q_HEADS, T, T), F32),
                        pltpu.VMEM((SWA_Q_HEADS, T // 2, T), F32)],
        compiler_params=pltpu.CompilerParams(
            dimension_semantics=("arbitrary", "arbitrary"), vmem_limit_bytes=VMEM_LIMIT),
        name="swa_attention",
    )(sinks, qt_a, k_a, k_a, vt_a, vt_a, swa_bias)


def _moba_kernel(qt_ref, k_ref, vt_ref, bias_ref, o_ref,
                 kaug_ref, vaug_ref, kmean_ref, qaug_ref, acc_ref,
                 s00_ref, s01_ref, s10_ref, s11_ref, *, nblk):
    s_refs = ((s00_ref, s01_ref), (s10_ref, s11_ref))
    heads = range(2)
    head_rows = [slice(hl * HEAD_DIM, (hl + 1) * HEAD_DIM) for hl in heads]

    rows = KEY_GROUP * T
    sel_r = lax.broadcasted_iota(jnp.int32, (2 * HEAD_DIM, AUG), 0)
    sel_c = lax.broadcasted_iota(jnp.int32, (2 * HEAD_DIM, AUG), 1)
    picks = [jnp.where((sel_r == sel_c + hl * HEAD_DIM) & (sel_c < HEAD_DIM), 1.0, 0.0).astype(BF16)
             for hl in heads]
    lane = lax.broadcasted_iota(jnp.int32, (rows, AUG), 1)
    row_blk = lax.shift_right_logical(lax.broadcasted_iota(jnp.int32, (rows, AUG), 0), T.bit_length() - 1)
    pad = _ones_pad_rows(T)

    def prep(c, carry):
        row0 = pl.multiple_of(c * rows, rows)
        blk0 = pl.multiple_of(c * KEY_GROUP, KEY_GROUP)
        kb = k_ref[0, pl.ds(row0, rows), :]
        onehot = jnp.where(lane == HEAD_DIM + blk0 + row_blk, 1.0, 0.0)
        for hl in heads:
            kp = jnp.dot(kb, picks[hl], preferred_element_type=F32)
            kaug_ref[hl, pl.ds(row0, rows), :] = (kp + onehot).astype(BF16)
            kmean_ref[hl, pl.ds(blk0, KEY_GROUP), :] = (
                jnp.sum(kp.reshape(KEY_GROUP, T, AUG), axis=1) * (1.0 / T))
            for r in range(KEY_GROUP):
                vaug_ref[hl, c * (KEY_GROUP // 2) + r // 2, :, (r % 2) * T:(r % 2 + 1) * T] = (
                    jnp.concatenate([vt_ref[0, blk0 + r, head_rows[hl], :], pad], axis=0))
        return carry

    lax.fori_loop(0, nblk // KEY_GROUP, prep, 0)

    width = QUERY_GROUP * T
    blk_id = lax.broadcasted_iota(jnp.int32, (nblk, width), 0)
    lane_blk = lax.shift_right_logical(lax.broadcasted_iota(jnp.int32, (nblk, width), 1), T.bit_length() - 1)
    zeros_q = jnp.zeros((AUG - HEAD_DIM, T), BF16)
    zeros_tail = jnp.zeros((AUG - HEAD_DIM - nblk, T), BF16)
    kmeans = []
    for hl in heads:
        km = kmean_ref[hl]
        km_hi = km.astype(BF16)
        kmeans.append((km_hi, (km - km_hi.astype(F32)).astype(BF16)))

    def route(c, carry):
        blk0 = pl.multiple_of(c * QUERY_GROUP, QUERY_GROUP)
        q_blk = blk0 + lane_blk
        for hl in heads:
            km_hi, km_lo = kmeans[hl]
            qts = [qt_ref[0, blk0 + x, head_rows[hl], :] for x in range(QUERY_GROUP)]
            scs = []
            for qt in qts:
                qs = jnp.concatenate([qt, zeros_q], axis=0)
                scs.append(jnp.dot(km_hi, qs, preferred_element_type=F32)
                           + jnp.dot(km_lo, qs, preferred_element_type=F32))
            sc = jnp.concatenate(scs, axis=1)
            sc = jnp.where(blk_id < q_blk, sc, -jnp.inf)
            chosen = blk_id == q_blk
            for _ in range(MOBA_TOPK):
                mx = jnp.max(sc, axis=0, keepdims=True)
                first = jnp.min(jnp.where(sc == mx, blk_id, nblk), axis=0, keepdims=True)
                hit = (blk_id == first) & (mx > -jnp.inf)
                chosen = chosen | hit
                sc = jnp.where(hit, -jnp.inf, sc)
            sel_bias = jnp.where(chosen, 0.0, NEG).astype(BF16)
            for x in range(QUERY_GROUP):
                qaug_ref[hl, blk0 + x] = jnp.concatenate(
                    [qts[x], sel_bias[:, x * T:(x + 1) * T], zeros_tail], axis=0)
        return carry

    lax.fori_loop(0, nblk // QUERY_GROUP, route, 0)

    def next_item(i, u):
        last = u + 1 == (i + 2) // 2
        return jnp.where(last, i + 1, i), jnp.where(last, 0, u + 1)

    def score(hl, slot, i, u, m):
        i = jnp.minimum(i, nblk - 1)
        row0 = pl.multiple_of(u * 2 * T, 2 * T)
        st = jnp.dot(kaug_ref[hl, pl.ds(row0, 2 * T), :], qaug_ref[hl, i], preferred_element_type=F32)
        s_even = st[0:T] + bias_ref[hl, jnp.clip(i - 2 * u, 0, N_NEAR)]
        s_odd = st[T:] + bias_ref[hl, jnp.clip(i - 2 * u - 1, 0, N_NEAR)]
        s_refs[hl][slot][0:T, :] = s_even
        s_refs[hl][slot][T:, :] = s_odd
        m = jnp.where(u == 0, NEG, m)
        top = jnp.maximum(jnp.max(s_even, axis=0, keepdims=True), jnp.max(s_odd, axis=0, keepdims=True))
        m_new = jnp.maximum(m, top)
        return m_new, jnp.exp2(m - m_new)

    def accumulate(hl, slot, i, u, m, alpha, acc):
        p = jnp.exp2(s_refs[hl][slot][...] - m).astype(BF16)
        acc = acc * alpha + jnp.dot(vaug_ref[hl, u], p, preferred_element_type=F32)
        acc_ref[hl, i] = acc
        return acc

    def item_step(slot, i, u, state):
        i_nxt, u_nxt = next_item(i, u)
        nxt = [score(hl, 1 - slot, i_nxt, u_nxt, state[hl][0]) for hl in heads]
        state = tuple(nxt[hl] + (accumulate(hl, slot, i, u, *state[hl]),) for hl in heads)
        return i_nxt, u_nxt, state

    def item_group(_, carry):
        i, u, state = carry
        for n in range(ITEM_UNROLL):
            i, u, state = item_step(n % 2, i, u, state)
        return i, u, state

    zero = jnp.int32(0)
    m0 = jnp.full((1, T), NEG, F32)
    state = tuple(score(hl, 0, zero, zero, m0) + (jnp.zeros((V_ROWS, T), F32),) for hl in heads)
    n_items = sum((i + 2) // 2 for i in range(nblk))
    assert n_items % ITEM_UNROLL == 0 and ITEM_UNROLL % 2 == 0
    lax.fori_loop(0, n_items // ITEM_UNROLL, item_group, (zero, zero, state))

    def finish(c, carry):
        for x in range(QUERY_GROUP):
            i = c * QUERY_GROUP + x
            outs = []
            for hl in heads:
                acc = acc_ref[hl, i]
                outs.append(acc[:HEAD_DIM] / acc[HEAD_DIM:HEAD_DIM + 1])
            o_ref[0, pl.ds(pl.multiple_of(i * T, T), T), :] = jnp.concatenate(outs, axis=0).T.astype(BF16)
        return carry

    lax.fori_loop(0, nblk // QUERY_GROUP, finish, 0)


def _moba(qt_b, k_b, vt_b, moba_bias):
    B, nblk = qt_b.shape[0], qt_b.shape[1]
    S = nblk * T
    assert HEAD_DIM + nblk <= AUG, "block one-hot must fit the augmented contraction width"
    assert nblk % KEY_GROUP == 0 and nblk % QUERY_GROUP == 0 and nblk % 2 == 0
    pair = 2 * HEAD_DIM
    return pl.pallas_call(
        functools.partial(_moba_kernel, nblk=nblk),
        grid=(B, MOBA_HEADS // 2),
        in_specs=[
            pl.BlockSpec((1, nblk, pair, T), lambda b, h: (b, 0, h, 0)),
            pl.BlockSpec((1, S, pair), lambda b, h: (b, 0, h)),
            pl.BlockSpec((1, nblk, pair, T), lambda b, h: (b, 0, h, 0)),
            pl.BlockSpec((2, N_NEAR + 1, T, T), lambda b, h: (h, 0, 0, 0)),
        ],
        out_specs=pl.BlockSpec((1, S, pair), lambda b, h: (b, 0, h)),
        out_shape=jax.ShapeDtypeStruct((B, S, MOBA_WIDTH), BF16),
        scratch_shapes=[
            pltpu.VMEM((2, S, AUG), BF16),
            pltpu.VMEM((2, nblk // 2, V_ROWS, 2 * T), BF16),
            pltpu.VMEM((2, nblk, AUG), F32),
            pltpu.VMEM((2, nblk, AUG, T), BF16),
            pltpu.VMEM((2, nblk, V_ROWS, T), F32),
        ] + [pltpu.VMEM((2 * T, T), F32)] * 4,
        compiler_params=pltpu.CompilerParams(
            dimension_semantics=("arbitrary", "arbitrary"), vmem_limit_bytes=VMEM_LIMIT),
        name="moba_attention",
    )(qt_b, k_b, vt_b, moba_bias)


def _sigmoid(v):
    return 1.0 / (1.0 + jnp.exp(-v))


def _rms(v, gain):
    r = lax.rsqrt(jnp.mean(v * v, axis=-1, keepdims=True) + RMS_EPS)
    return (v * r) * gain


def _outproj_kernel(x_ref, ao_ref, bo_ref, ga_ref, gb_ref, p_ref,
                    wo_ref, pg_ref, wg_ref, wp_ref, fg_ref, o_ref, *, final):
    chunk = OUT_ROW_TILE // OUT_CHUNKS
    chunks = [slice(c * chunk, (c + 1) * chunk) for c in range(OUT_CHUNKS)]

    def mix(rows):
        ga = ga_ref[0, rows, :].astype(F32)
        gb = gb_ref[0, rows, :].astype(F32)
        ma = (ao_ref[0, rows, :].astype(F32) * (ga * _sigmoid(ga))).astype(BF16)
        mb = (bo_ref[0, rows, :].astype(F32) * (gb * _sigmoid(gb))).astype(BF16)
        return (x_ref[0, rows, :]
                + jnp.dot(ma, wo_ref[0:SWA_WIDTH, :], preferred_element_type=F32)
                + jnp.dot(mb, wo_ref[SWA_WIDTH:, :], preferred_element_type=F32))

    def gate_up(rows, x1):
        n = _rms(x1, pg_ref[...]).astype(BF16)
        gate = _sigmoid(jnp.dot(n, wg_ref[...], preferred_element_type=F32))
        up = jnp.dot(p_ref[0, rows, :].astype(BF16), wp_ref[...], preferred_element_type=F32)
        return x1 + up * gate

    x1s = [mix(rows) for rows in chunks]
    ys = [gate_up(rows, x1) for rows, x1 in zip(chunks, x1s)]
    for rows, y in zip(chunks, ys):
        o_ref[0, rows, :] = _rms(y, fg_ref[...]) if final else y


def _outproj(x, a_out, b_out, g_a, g_b, p, w_out, ple_gain, w_gate, w_proj, final_gain, final):
    B, S, D = x.shape
    nt = S // OUT_ROW_TILE
    ple = p.shape[-1]

    def tok(width):
        return pl.BlockSpec((1, OUT_ROW_TILE, width), lambda b, t: (b, t, 0))

    def whole(shape):
        return pl.BlockSpec(shape, lambda b, t: (0,) * len(shape))

    return pl.pallas_call(
        functools.partial(_outproj_kernel, final=final),
        grid=(B, nt),
        in_specs=[tok(D), tok(SWA_WIDTH), tok(MOBA_WIDTH), tok(SWA_WIDTH), tok(MOBA_WIDTH), tok(ple),
                  whole((D, D)), whole((1, D)), whole((D, D)), whole((ple, D)), whole((1, D))],
        out_specs=tok(D),
        out_shape=jax.ShapeDtypeStruct((B, S, D), F32),
        compiler_params=pltpu.CompilerParams(
            dimension_semantics=("arbitrary", "arbitrary"), vmem_limit_bytes=VMEM_LIMIT),
        name="outproj_ple",
    )(x, a_out, b_out, g_a, g_b, p, w_out, ple_gain, w_gate, w_proj, final_gain)


def _wprep_kernel(win_ref, wo_ref, wg_ref, wp_ref, wnn_ref, wt_ref, wob_ref, wgb_ref, wpb_ref):
    widths = (SWA_WIDTH, SWA_KV_WIDTH, SWA_KV_WIDTH, SWA_WIDTH,
              MOBA_WIDTH, MOBA_WIDTH, MOBA_WIDTH, MOBA_WIDTH)
    offs = [0]
    for wd in widths:
        offs.append(offs[-1] + wd)
    w = win_ref[...]
    a_q, a_k, a_v, a_g, b_q, b_k, b_v, b_g = (w[:, offs[n]:offs[n + 1]] for n in range(8))
    scale = HEAD_DIM ** -0.5 * LOG2E
    wnn_ref[...] = jnp.concatenate([a_k, a_g, b_k, b_g], axis=1).astype(BF16)
    wt_ref[...] = jnp.concatenate([a_q * scale, a_v, b_q * scale, b_v], axis=1).T.astype(BF16)
    wob_ref[...] = wo_ref[...].astype(BF16)
    wgb_ref[...] = wg_ref[...].astype(BF16)
    wpb_ref[...] = wp_ref[...].astype(BF16)


def _prepare_weights(w_in, w_out, w_gate, w_proj):
    D, n_in = w_in.shape
    ple = w_proj.shape[0]
    steps = D // W_ROWS
    n_half = n_in // 2
    rows = lambda width: pl.BlockSpec((W_ROWS, width), lambda k: (k, 0))
    return pl.pallas_call(
        _wprep_kernel,
        grid=(steps,),
        in_specs=[rows(n_in), rows(D), rows(D), pl.BlockSpec((ple // steps, D), lambda k: (k, 0))],
        out_specs=[rows(n_half), pl.BlockSpec((n_half, W_ROWS), lambda k: (0, k)), rows(D), rows(D),
                   pl.BlockSpec((ple // steps, D), lambda k: (k, 0))],
        out_shape=[jax.ShapeDtypeStruct((D, n_half), BF16), jax.ShapeDtypeStruct((n_half, D), BF16),
                   jax.ShapeDtypeStruct((D, D), BF16), jax.ShapeDtypeStruct((D, D), BF16),
                   jax.ShapeDtypeStruct((ple, D), BF16)],
        name="weight_prep",
    )(w_in, w_out, w_gate, w_proj)


def kernel(x, p, norm_in, w_in, sinks, rel_bias, w_out, ple_norm, w_ple_gate, w_ple_proj, final_norm):
    B, S, D = x.shape
    depth = p.shape[0]
    assert S % ROW_TILE == 0 and ROW_TILE % T == 0 and S % OUT_ROW_TILE == 0
    moba_bias, swa_bias = _bias_tables(rel_bias)
    for i in range(depth):
        w_nn, w_t, w_o, w_g, w_p = _prepare_weights(w_in[i], w_out[i], w_ple_gate[i], w_ple_proj[i])
        k_a, g_a, k_b, g_b, qt_a, vt_a, qt_b, vt_b = _inproj(x, norm_in[i][None, :], w_nn, w_t)
        a_out = _swa(sinks[i], qt_a, k_a, vt_a, swa_bias)
        b_out = _moba(qt_b, k_b, vt_b, moba_bias)
        x = _outproj(x, a_out, b_out, g_a, g_b, p[i], w_o, ple_norm[i][None, :], w_g, w_p,
                     final_norm[None, :], final=(i == depth - 1))
    return x
```

```python
import functools
import math

import jax
import jax.numpy as jnp
from jax import lax
from jax.experimental import pallas as pl
from jax.experimental.pallas import tpu as pltpu

F32 = jnp.float32
BF16 = jnp.bfloat16

HEAD_DIM = 64
SWA_Q_HEADS = 8
SWA_KV_HEADS = 2
MOBA_HEADS = 8
WINDOW = 128
MOBA_BLOCK = 256
MOBA_TOPK = 3
NUM_BUCKETS = 32
MAX_DISTANCE = 1024
RMS_EPS = 1e-6

SWA_WIDTH = SWA_Q_HEADS * HEAD_DIM
SWA_KV_WIDTH = SWA_KV_HEADS * HEAD_DIM
MOBA_WIDTH = MOBA_HEADS * HEAD_DIM
SWA_GROUP = SWA_Q_HEADS // SWA_KV_HEADS

T = MOBA_BLOCK
AUG = 128
V_ROWS = HEAD_DIM + 16
NEG = -1e30
LOG2E = math.log2(math.e)
N_NEAR = -(-(MAX_DISTANCE + T) // T)
KEY_GROUP = 8
QUERY_GROUP = 4
ITEM_UNROLL = 16
assert WINDOW <= T // 2
SWA_SPLIT = 2
SWA_UNROLL = 8
ROW_TILE = 512
W_ROWS = 128
OUT_ROW_TILE = 1024
OUT_CHUNKS = 4
VMEM_LIMIT = 56 * 1024 * 1024


def _bucket_starts():
    max_exact = NUM_BUCKETS // 2
    starts = list(range(max_exact))
    for j in range(NUM_BUCKETS - max_exact):
        edge = max_exact * (MAX_DISTANCE / max_exact) ** (j / (NUM_BUCKETS - max_exact))
        starts.append(math.ceil(edge - 1e-9))
    return starts


BUCKET_STARTS = _bucket_starts()


def _bias_of_distance(dist, table, dist_lo, dist_hi):
    bucket = lambda n: max(b for b in range(NUM_BUCKETS) if BUCKET_STARTS[b] <= max(n, 0))
    lo, hi = bucket(dist_lo), bucket(dist_hi)
    out = jnp.full(dist.shape, table(lo), F32)
    for b in range(lo + 1, hi + 1):
        out = jnp.where(dist >= BUCKET_STARTS[b], table(b), out)
    return out


def _bias_kernel(rel_ref, moba_ref, swa_ref):
    h = pl.program_id(0)
    kk = lax.broadcasted_iota(jnp.int32, (T, T), 0)
    qq = lax.broadcasted_iota(jnp.int32, (T, T), 1)
    rel = qq - kk

    def lookup(dist, col, dist_lo, dist_hi):
        return _bias_of_distance(dist, lambda b: rel_ref[b, col] * LOG2E, dist_lo, dist_hi)

    for d in range(N_NEAR):
        dist = rel + d * T
        bias = lookup(dist, SWA_Q_HEADS + h, max(d * T - (T - 1), 0), d * T + (T - 1))
        if d == 0:
            bias = jnp.where(dist >= 0, bias, NEG)
        moba_ref[0, d] = bias
    far = jnp.full((T, T), N_NEAR * T, jnp.int32)
    moba_ref[0, N_NEAR] = lookup(far, SWA_Q_HEADS + h, N_NEAR * T, N_NEAR * T)

    for d in range(2):
        dist = rel + d * T
        bias = lookup(dist, h, 0, WINDOW - 1)
        valid = (dist >= 0) & (dist < WINDOW)
        swa_ref[0, d] = jnp.where(valid, bias, NEG)


def _bias_tables(rel_bias):
    return pl.pallas_call(
        _bias_kernel,
        grid=(MOBA_HEADS,),
        in_specs=[pl.BlockSpec(memory_space=pltpu.SMEM)],
        out_specs=[
            pl.BlockSpec((1, N_NEAR + 1, T, T), lambda h: (h, 0, 0, 0)),
            pl.BlockSpec((1, 2, T, T), lambda h: (h, 0, 0, 0)),
        ],
        out_shape=[
            jax.ShapeDtypeStruct((MOBA_HEADS, N_NEAR + 1, T, T), F32),
            jax.ShapeDtypeStruct((SWA_Q_HEADS, 2, T, T), F32),
        ],
        name="bias_tables",
    )(rel_bias)


def _inproj_kernel(x_ref, g_ref, wnn_ref, wt_ref,
                   ka_ref, ga_ref, kb_ref, gb_ref,
                   qta_ref, vta_ref, qtb_ref, vtb_ref):
    x = x_ref[0]
    r = lax.rsqrt(jnp.mean(x * x, axis=-1, keepdims=True) + RMS_EPS)
    h = ((x * r) * g_ref[...]).astype(BF16)

    tok = jnp.dot(h, wnn_ref[...], preferred_element_type=F32)
    o = 0
    for ref, width in ((ka_ref, SWA_KV_WIDTH), (ga_ref, SWA_WIDTH),
                       (kb_ref, MOBA_WIDTH), (gb_ref, MOBA_WIDTH)):
        ref[0] = tok[:, o:o + width].astype(BF16)
        o += width

    feat = lax.dot_general(wt_ref[...], h, (((1,), (1,)), ((), ())),
                           preferred_element_type=F32)
    o = 0
    for ref, width in ((qta_ref, SWA_WIDTH), (vta_ref, SWA_KV_WIDTH),
                       (qtb_ref, MOBA_WIDTH), (vtb_ref, MOBA_WIDTH)):
        for c in range(ROW_TILE // T):
            ref[0, c] = feat[o:o + width, c * T:(c + 1) * T].astype(BF16)
        o += width


def _inproj(x, gain, w_nn, w_t):
    B, S, D = x.shape
    nt = S // ROW_TILE
    cpt = ROW_TILE // T
    n_nn = w_nn.shape[1]
    n_t = w_t.shape[0]

    def tok_spec(width):
        return pl.BlockSpec((1, ROW_TILE, width), lambda b, t: (b, t, 0))

    def feat_spec(width):
        return pl.BlockSpec((1, cpt, width, T), lambda b, t: (b, t, 0, 0))

    def tok_shape(width):
        return jax.ShapeDtypeStruct((B, S, width), BF16)

    def feat_shape(width):
        return jax.ShapeDtypeStruct((B, S // T, width, T), BF16)

    return pl.pallas_call(
        _inproj_kernel,
        grid=(B, nt),
        in_specs=[
            pl.BlockSpec((1, ROW_TILE, D), lambda b, t: (b, t, 0)),
            pl.BlockSpec((1, D), lambda b, t: (0, 0)),
            pl.BlockSpec((D, n_nn), lambda b, t: (0, 0)),
            pl.BlockSpec((n_t, D), lambda b, t: (0, 0)),
        ],
        out_specs=[tok_spec(SWA_KV_WIDTH), tok_spec(SWA_WIDTH), tok_spec(MOBA_WIDTH), tok_spec(MOBA_WIDTH),
                   feat_spec(SWA_WIDTH), feat_spec(SWA_KV_WIDTH), feat_spec(MOBA_WIDTH), feat_spec(MOBA_WIDTH)],
        out_shape=[tok_shape(SWA_KV_WIDTH), tok_shape(SWA_WIDTH), tok_shape(MOBA_WIDTH), tok_shape(MOBA_WIDTH),
                   feat_shape(SWA_WIDTH), feat_shape(SWA_KV_WIDTH), feat_shape(MOBA_WIDTH), feat_shape(MOBA_WIDTH)],
        compiler_params=pltpu.CompilerParams(
            dimension_semantics=("arbitrary", "arbitrary"), vmem_limit_bytes=VMEM_LIMIT),
        name="norm_inproj",
    )(x, gain, w_nn, w_t)


def _ones_pad_rows(width):
    row = lax.broadcasted_iota(jnp.int32, (V_ROWS - HEAD_DIM, width), 0)
    return jnp.where(row == 0, 1.0, 0.0).astype(BF16)


def _swa_kernel(sink_ref, qt_ref, k_ref, vt_ref, bias_ref, o_ref,
                cur0_ref, prev0_ref, cur1_ref, prev1_ref, *, per_step):
    bufs = ((cur0_ref, prev0_ref), (cur1_ref, prev1_ref))
    half = T // 2
    base = pl.program_id(1) * per_step
    zeros_q = jnp.zeros((HEAD_DIM, T), BF16)
    head_rows = [slice(h * HEAD_DIM, (h + 1) * HEAD_DIM) for h in range(SWA_Q_HEADS)]

    def score(slot, t):
        t = jnp.minimum(t, per_step - 1)
        blk = base + t
        row0 = pl.multiple_of(blk * T, T)
        k_cur = k_ref[0, pl.ds(row0, T), :]
        k_prev = k_ref[0, pl.ds(pl.multiple_of(jnp.maximum(row0 - half, 0), half), half), :]
        prev_pen = jnp.where(blk == 0, NEG, 0.0).astype(F32)
        maxes = []
        for head in range(SWA_Q_HEADS):
            parts = [zeros_q] * SWA_KV_HEADS
            parts[head // SWA_GROUP] = qt_ref[0, t, head_rows[head], :]
            q_pad = jnp.concatenate(parts, axis=0)
            s_cur = jnp.dot(k_cur, q_pad, preferred_element_type=F32) + bias_ref[head, 0]
            s_prev = (jnp.dot(k_prev, q_pad, preferred_element_type=F32)
                      + (bias_ref[head, 1, half:, :] + prev_pen))
            bufs[slot][0][head] = s_cur
            bufs[slot][1][head] = s_prev
            m = jnp.maximum(jnp.max(s_cur, axis=0, keepdims=True), jnp.max(s_prev, axis=0, keepdims=True))
            maxes.append(jnp.maximum(m, sink_ref[head] * LOG2E))
        return tuple(maxes)

    def finish(slot, t, maxes):
        blk = base + t
        blk_prev = jnp.maximum(blk - 1, 0)
        outs = []
        for g in range(SWA_KV_HEADS):
            v_cur = jnp.concatenate([vt_ref[0, blk, head_rows[g], :], _ones_pad_rows(T)], axis=0)
            v_prev = jnp.concatenate([vt_ref[0, blk_prev, head_rows[g], half:], _ones_pad_rows(half)], axis=0)
            for head in range(g * SWA_GROUP, (g + 1) * SWA_GROUP):
                m = maxes[head]
                p_cur = jnp.exp2(bufs[slot][0][head] - m).astype(BF16)
                p_prev = jnp.exp2(bufs[slot][1][head] - m).astype(BF16)
                acc = (jnp.dot(v_cur, p_cur, preferred_element_type=F32)
                       + jnp.dot(v_prev, p_prev, preferred_element_type=F32))
                den = acc[HEAD_DIM:HEAD_DIM + 1] + jnp.exp2(sink_ref[head] * LOG2E - m)
                outs.append(acc[:HEAD_DIM] / den)
        o_ref[0, pl.ds(pl.multiple_of(t * T, T), T), :] = jnp.concatenate(outs, axis=0).T.astype(BF16)

    def block_group(c, maxes):
        for n in range(SWA_UNROLL):
            t = c * SWA_UNROLL + n
            nxt = score((n + 1) % 2, t + 1)
            finish(n % 2, t, maxes)
            maxes = nxt
        return maxes

    lax.fori_loop(0, per_step // SWA_UNROLL, block_group, score(0, jnp.int32(0)))


def _swa(sinks, qt_a, k_a, vt_a, swa_bias):
    B, nblk = qt_a.shape[0], qt_a.shape[1]
    S = nblk * T
    per_step = nblk // SWA_SPLIT
    assert nblk % SWA_SPLIT == 0 and per_step % SWA_UNROLL == 0 and SWA_UNROLL % 2 == 0
    return pl.pallas_call(
        functools.partial(_swa_kernel, per_step=per_step),
        grid=(B, SWA_SPLIT),
        in_specs=[
            pl.BlockSpec(memory_space=pltpu.SMEM),
            pl.BlockSpec((1, per_step, SWA_WIDTH, T), lambda b, h: (b, h, 0, 0)),
            pl.BlockSpec((1, S, SWA_KV_WIDTH), lambda b, h: (b, 0, 0)),
            pl.BlockSpec((1, nblk, SWA_KV_WIDTH, T), lambda b, h: (b, 0, 0, 0)),
            pl.BlockSpec((SWA_Q_HEADS, 2, T, T), lambda b, h: (0, 0, 0, 0)),
        ],
        out_specs=pl.BlockSpec((1, per_step * T, SWA_WIDTH), lambda b, h: (b, h, 0)),
        out_shape=jax.ShapeDtypeStruct((B, S, SWA_WIDTH), BF16),
        scratch_shapes=[pltpu.VMEM((SWA_Q_HEADS, T, T), F32),
                        pltpu.VMEM((SWA_Q_HEADS, T // 2, T), F32)] * 2,
        compiler_params=pltpu.CompilerParams(
            dimension_semantics=("arbitrary", "arbitrary"), vmem_limit_bytes=VMEM_LIMIT),
        name="swa_attention",
    )(sinks, qt_a, k_a, vt_a, swa_bias)


def _moba_kernel(qt_ref, k_ref, vt_ref, bias_ref, o_ref,
                 kaug_ref, vaug_ref, kmean_ref, qaug_ref, acc_ref,
                 s00_ref, s01_ref, s10_ref, s11_ref, *, nblk):
    s_refs = ((s00_ref, s01_ref), (s10_ref, s11_ref))
    heads = range(2)
    head_rows = [slice(hl * HEAD_DIM, (hl + 1) * HEAD_DIM) for hl in heads]

    rows = KEY_GROUP * T
    sel_r = lax.broadcasted_iota(jnp.int32, (2 * HEAD_DIM, AUG), 0)
    sel_c = lax.broadcasted_iota(jnp.int32, (2 * HEAD_DIM, AUG), 1)
    picks = [jnp.where((sel_r == sel_c + hl * HEAD_DIM) & (sel_c < HEAD_DIM), 1.0, 0.0).astype(BF16)
             for hl in heads]
    lane = lax.broadcasted_iota(jnp.int32, (rows, AUG), 1)
    row_blk = lax.shift_right_logical(lax.broadcasted_iota(jnp.int32, (rows, AUG), 0), T.bit_length() - 1)
    pad = _ones_pad_rows(T)

    def prep(c, carry):
        row0 = pl.multiple_of(c * rows, rows)
        blk0 = pl.multiple_of(c * KEY_GROUP, KEY_GROUP)
        kb = k_ref[0, pl.ds(row0, rows), :]
        onehot = jnp.where(lane == HEAD_DIM + blk0 + row_blk, 1.0, 0.0)
        for hl in heads:
            kp = jnp.dot(kb, picks[hl], preferred_element_type=F32)
            kaug_ref[hl, pl.ds(row0, rows), :] = (kp + onehot).astype(BF16)
            kmean_ref[hl, pl.ds(blk0, KEY_GROUP), :] = (
                jnp.sum(kp.reshape(KEY_GROUP, T, AUG), axis=1) * (1.0 / T))
            for r in range(KEY_GROUP):
                vaug_ref[hl, c * (KEY_GROUP // 2) + r // 2, :, (r % 2) * T:(r % 2 + 1) * T] = (
                    jnp.concatenate([vt_ref[0, blk0 + r, head_rows[hl], :], pad], axis=0))
        return carry

    lax.fori_loop(0, nblk // KEY_GROUP, prep, 0)

    width = QUERY_GROUP * T
    blk_id = lax.broadcasted_iota(jnp.int32, (nblk, width), 0)
    lane_blk = lax.shift_right_logical(lax.broadcasted_iota(jnp.int32, (nblk, width), 1), T.bit_length() - 1)
    zeros_q = jnp.zeros((AUG - HEAD_DIM, T), BF16)
    zeros_tail = jnp.zeros((AUG - HEAD_DIM - nblk, T), BF16)
    kmeans = []
    for hl in heads:
        km = kmean_ref[hl]
        km_hi = km.astype(BF16)
        kmeans.append((km_hi, (km - km_hi.astype(F32)).astype(BF16)))

    def route(c, carry):
        blk0 = pl.multiple_of(c * QUERY_GROUP, QUERY_GROUP)
        q_blk = blk0 + lane_blk
        for hl in heads:
            km_hi, km_lo = kmeans[hl]
            qts = [qt_ref[0, blk0 + x, head_rows[hl], :] for x in range(QUERY_GROUP)]
            scs = []
            for qt in qts:
                qs = jnp.concatenate([qt, zeros_q], axis=0)
                scs.append(jnp.dot(km_hi, qs, preferred_element_type=F32)
                           + jnp.dot(km_lo, qs, preferred_element_type=F32))
            sc = jnp.concatenate(scs, axis=1)
            sc = jnp.where(blk_id < q_blk, sc, -jnp.inf)
            chosen = blk_id == q_blk
            for _ in range(MOBA_TOPK):
                mx = jnp.max(sc, axis=0, keepdims=True)
                first = jnp.min(jnp.where(sc == mx, blk_id, nblk), axis=0, keepdims=True)
                hit = (blk_id == first) & (mx > -jnp.inf)
                chosen = chosen | hit
                sc = jnp.where(hit, -jnp.inf, sc)
            sel_bias = jnp.where(chosen, 0.0, NEG).astype(BF16)
            for x in range(QUERY_GROUP):
                qaug_ref[hl, blk0 + x] = jnp.concatenate(
                    [qts[x], sel_bias[:, x * T:(x + 1) * T], zeros_tail], axis=0)
        return carry

    lax.fori_loop(0, nblk // QUERY_GROUP, route, 0)

    def next_item(i, u):
        last = u + 1 == (i + 2) // 2
        return jnp.where(last, i + 1, i), jnp.where(last, 0, u + 1)

    def score(hl, slot, i, u, m):
        i = jnp.minimum(i, nblk - 1)
        row0 = pl.multiple_of(u * 2 * T, 2 * T)
        st = jnp.dot(kaug_ref[hl, pl.ds(row0, 2 * T), :], qaug_ref[hl, i], preferred_element_type=F32)
        s_even = st[0:T] + bias_ref[hl, jnp.clip(i - 2 * u, 0, N_NEAR)]
        s_odd = st[T:] + bias_ref[hl, jnp.clip(i - 2 * u - 1, 0, N_NEAR)]
        s_refs[hl][slot][0:T, :] = s_even
        s_refs[hl][slot][T:, :] = s_odd
        m = jnp.where(u == 0, NEG, m)
        top = jnp.maximum(jnp.max(s_even, axis=0, keepdims=True), jnp.max(s_odd, axis=0, keepdims=True))
        m_new = jnp.maximum(m, top)
        return m_new, jnp.exp2(m - m_new)

    def accumulate(hl, slot, i, u, m, alpha, acc):
        p = jnp.exp2(s_refs[hl][slot][...] - m).astype(BF16)
        acc = acc * alpha + jnp.dot(vaug_ref[hl, u], p, preferred_element_type=F32)
        acc_ref[hl, i] = acc
        return acc

    def item_step(slot, i, u, state):
        i_nxt, u_nxt = next_item(i, u)
        nxt = [score(hl, 1 - slot, i_nxt, u_nxt, state[hl][0]) for hl in heads]
        state = tuple(nxt[hl] + (accumulate(hl, slot, i, u, *state[hl]),) for hl in heads)
        return i_nxt, u_nxt, state

    def item_group(_, carry):
        i, u, state = carry
        for n in range(ITEM_UNROLL):
            i, u, state = item_step(n % 2, i, u, state)
        return i, u, state

    zero = jnp.int32(0)
    m0 = jnp.full((1, T), NEG, F32)
    state = tuple(score(hl, 0, zero, zero, m0) + (jnp.zeros((V_ROWS, T), F32),) for hl in heads)
    n_items = sum((i + 2) // 2 for i in range(nblk))
    assert n_items % ITEM_UNROLL == 0 and ITEM_UNROLL % 2 == 0
    lax.fori_loop(0, n_items // ITEM_UNROLL, item_group, (zero, zero, state))

    def finish(c, carry):
        for x in range(QUERY_GROUP):
            i = c * QUERY_GROUP + x
            outs = []
            for hl in heads:
                acc = acc_ref[hl, i]
                outs.append(acc[:HEAD_DIM] / acc[HEAD_DIM:HEAD_DIM + 1])
            o_ref[0, pl.ds(pl.multiple_of(i * T, T), T), :] = jnp.concatenate(outs, axis=0).T.astype(BF16)
        return carry

    lax.fori_loop(0, nblk // QUERY_GROUP, finish, 0)


def _moba(qt_b, k_b, vt_b, moba_bias):
    B, nblk = qt_b.shape[0], qt_b.shape[1]
    S = nblk * T
    assert HEAD_DIM + nblk <= AUG, "block one-hot must fit the augmented contraction width"
    assert nblk % KEY_GROUP == 0 and nblk % QUERY_GROUP == 0 and nblk % 2 == 0
    pair = 2 * HEAD_DIM
    return pl.pallas_call(
        functools.partial(_moba_kernel, nblk=nblk),
        grid=(B, MOBA_HEADS // 2),
        in_specs=[
            pl.BlockSpec((1, nblk, pair, T), lambda b, h: (b, 0, h, 0)),
            pl.BlockSpec((1, S, pair), lambda b, h: (b, 0, h)),
            pl.BlockSpec((1, nblk, pair, T), lambda b, h: (b, 0, h, 0)),
            pl.BlockSpec((2, N_NEAR + 1, T, T), lambda b, h: (h, 0, 0, 0)),
        ],
        out_specs=pl.BlockSpec((1, S, pair), lambda b, h: (b, 0, h)),
        out_shape=jax.ShapeDtypeStruct((B, S, MOBA_WIDTH), BF16),
        scratch_shapes=[
            pltpu.VMEM((2, S, AUG), BF16),
            pltpu.VMEM((2, nblk // 2, V_ROWS, 2 * T), BF16),
            pltpu.VMEM((2, nblk, AUG), F32),
            pltpu.VMEM((2, nblk, AUG, T), BF16),
            pltpu.VMEM((2, nblk, V_ROWS, T), F32),
        ] + [pltpu.VMEM((2 * T, T), F32)] * 4,
        compiler_params=pltpu.CompilerParams(
            dimension_semantics=("arbitrary", "arbitrary"), vmem_limit_bytes=VMEM_LIMIT),
        name="moba_attention",
    )(qt_b, k_b, vt_b, moba_bias)


def _sigmoid(v):
    return 1.0 / (1.0 + jnp.exp(-v))


def _rms(v, gain):
    r = lax.rsqrt(jnp.mean(v * v, axis=-1, keepdims=True) + RMS_EPS)
    return (v * r) * gain


def _outproj_kernel(x_ref, ao_ref, bo_ref, ga_ref, gb_ref, p_ref,
                    wo_ref, pg_ref, wg_ref, wp_ref, fg_ref, o_ref, *, final):
    chunk = OUT_ROW_TILE // OUT_CHUNKS
    chunks = [slice(c * chunk, (c + 1) * chunk) for c in range(OUT_CHUNKS)]

    def mix(rows):
        ga = ga_ref[0, rows, :].astype(F32)
        gb = gb_ref[0, rows, :].astype(F32)
        ma = (ao_ref[0, rows, :].astype(F32) * (ga * _sigmoid(ga))).astype(BF16)
        mb = (bo_ref[0, rows, :].astype(F32) * (gb * _sigmoid(gb))).astype(BF16)
        return (x_ref[0, rows, :]
                + jnp.dot(ma, wo_ref[0:SWA_WIDTH, :], preferred_element_type=F32)
                + jnp.dot(mb, wo_ref[SWA_WIDTH:, :], preferred_element_type=F32))

    def gate_up(rows, x1):
        n = _rms(x1, pg_ref[...]).astype(BF16)
        gate = _sigmoid(jnp.dot(n, wg_ref[...], preferred_element_type=F32))
        up = jnp.dot(p_ref[0, rows, :].astype(BF16), wp_ref[...], preferred_element_type=F32)
        return x1 + up * gate

    x1s = [mix(rows) for rows in chunks]
    ys = [gate_up(rows, x1) for rows, x1 in zip(chunks, x1s)]
    for rows, y in zip(chunks, ys):
        o_ref[0, rows, :] = _rms(y, fg_ref[...]) if final else y


def _outproj(x, a_out, b_out, g_a, g_b, p, w_out, ple_gain, w_gate, w_proj, final_gain, final):
    B, S, D = x.shape
    nt = S // OUT_ROW_TILE
    ple = p.shape[-1]

    def tok(width):
        return pl.BlockSpec((1, OUT_ROW_TILE, width), lambda b, t: (b, t, 0))

    def whole(shape):
        return pl.BlockSpec(shape, lambda b, t: (0,) * len(shape))

    return pl.pallas_call(
        functools.partial(_outproj_kernel, final=final),
        grid=(B, nt),
        in_specs=[tok(D), tok(SWA_WIDTH), tok(MOBA_WIDTH), tok(SWA_WIDTH), tok(MOBA_WIDTH), tok(ple),
                  whole((D, D)), whole((1, D)), whole((D, D)), whole((ple, D)), whole((1, D))],
        out_specs=tok(D),
        out_shape=jax.ShapeDtypeStruct((B, S, D), F32),
        compiler_params=pltpu.CompilerParams(
            dimension_semantics=("arbitrary", "arbitrary"), vmem_limit_bytes=VMEM_LIMIT),
        name="outproj_ple",
    )(x, a_out, b_out, g_a, g_b, p, w_out, ple_gain, w_gate, w_proj, final_gain)


def _wprep_kernel(win_ref, wo_ref, wg_ref, wp_ref, wnn_ref, wt_ref, wob_ref, wgb_ref, wpb_ref):
    widths = (SWA_WIDTH, SWA_KV_WIDTH, SWA_KV_WIDTH, SWA_WIDTH,
              MOBA_WIDTH, MOBA_WIDTH, MOBA_WIDTH, MOBA_WIDTH)
    offs = [0]
    for wd in widths:
        offs.append(offs[-1] + wd)
    w = win_ref[...]
    a_q, a_k, a_v, a_g, b_q, b_k, b_v, b_g = (w[:, offs[n]:offs[n + 1]] for n in range(8))
    scale = HEAD_DIM ** -0.5 * LOG2E
    wnn_ref[...] = jnp.concatenate([a_k, a_g, b_k, b_g], axis=1).astype(BF16)
    wt_ref[...] = jnp.concatenate([a_q * scale, a_v, b_q * scale, b_v], axis=1).T.astype(BF16)
    wob_ref[...] = wo_ref[...].astype(BF16)
    wgb_ref[...] = wg_ref[...].astype(BF16)
    wpb_ref[...] = wp_ref[...].astype(BF16)


def _prepare_weights(w_in, w_out, w_gate, w_proj):
    D, n_in = w_in.shape
    ple = w_proj.shape[0]
    steps = D // W_ROWS
    n_half = n_in // 2
    rows = lambda width: pl.BlockSpec((W_ROWS, width), lambda k: (k, 0))
    return pl.pallas_call(
        _wprep_kernel,
        grid=(steps,),
        in_specs=[rows(n_in), rows(D), rows(D), pl.BlockSpec((ple // steps, D), lambda k: (k, 0))],
        out_specs=[rows(n_half), pl.BlockSpec((n_half, W_ROWS), lambda k: (0, k)), rows(D), rows(D),
                   pl.BlockSpec((ple // steps, D), lambda k: (k, 0))],
        out_shape=[jax.ShapeDtypeStruct((D, n_half), BF16), jax.ShapeDtypeStruct((n_half, D), BF16),
                   jax.ShapeDtypeStruct((D, D), BF16), jax.ShapeDtypeStruct((D, D), BF16),
                   jax.ShapeDtypeStruct((ple, D), BF16)],
        name="weight_prep",
    )(w_in, w_out, w_gate, w_proj)


def kernel(x, p, norm_in, w_in, sinks, rel_bias, w_out, ple_norm, w_ple_gate, w_ple_proj, final_norm):
    B, S, D = x.shape
    depth = p.shape[0]
    assert S % ROW_TILE == 0 and ROW_TILE % T == 0 and S % OUT_ROW_TILE == 0
    moba_bias, swa_bias = _bias_tables(rel_bias)
    for i in range(depth):
        w_nn, w_t, w_o, w_g, w_p = _prepare_weights(w_in[i], w_out[i], w_ple_gate[i], w_ple_proj[i])
        k_a, g_a, k_b, g_b, qt_a, vt_a, qt_b, vt_b = _inproj(x, norm_in[i][None, :], w_nn, w_t)
        a_out = _swa(sinks[i], qt_a, k_a, vt_a, swa_bias)
        b_out = _moba(qt_b, k_b, vt_b, moba_bias)
        x = _outproj(x, a_out, b_out, g_a, g_b, p[i], w_o, ple_norm[i][None, :], w_g, w_p,
                     final_norm[None, :], final=(i == depth - 1))
    return x
```

```python
import functools
import math

import jax
import jax.numpy as jnp
from jax import lax
from jax.experimental import pallas as pl
from jax.experimental.pallas import tpu as pltpu

F32 = jnp.float32
BF16 = jnp.bfloat16

HEAD_DIM = 64
SWA_Q_HEADS = 8
SWA_KV_HEADS = 2
MOBA_HEADS = 8
WINDOW = 128
MOBA_BLOCK = 256
MOBA_TOPK = 3
NUM_BUCKETS = 32
MAX_DISTANCE = 1024
RMS_EPS = 1e-6

SWA_WIDTH = SWA_Q_HEADS * HEAD_DIM
SWA_KV_WIDTH = SWA_KV_HEADS * HEAD_DIM
MOBA_WIDTH = MOBA_HEADS * HEAD_DIM
SWA_GROUP = SWA_Q_HEADS // SWA_KV_HEADS

T = MOBA_BLOCK
AUG = 128
V_ROWS = HEAD_DIM + 16
NEG = -1e30
LOG2E = math.log2(math.e)
N_NEAR = -(-(MAX_DISTANCE + T) // T)
KEY_GROUP = 8
QUERY_GROUP = 4
ITEM_UNROLL = 16
assert WINDOW <= T // 2
SWA_SPLIT = 2
SWA_UNROLL = 8
ROW_TILE = 512
W_ROWS = 128
OUT_ROW_TILE = 1024
OUT_CHUNKS = 8
VMEM_LIMIT = 56 * 1024 * 1024


def _bucket_starts():
    max_exact = NUM_BUCKETS // 2
    starts = list(range(max_exact))
    for j in range(NUM_BUCKETS - max_exact):
        edge = max_exact * (MAX_DISTANCE / max_exact) ** (j / (NUM_BUCKETS - max_exact))
        starts.append(math.ceil(edge - 1e-9))
    return starts


BUCKET_STARTS = _bucket_starts()


def _bias_of_distance(dist, table, dist_lo, dist_hi):
    bucket = lambda n: max(b for b in range(NUM_BUCKETS) if BUCKET_STARTS[b] <= max(n, 0))
    lo, hi = bucket(dist_lo), bucket(dist_hi)
    out = jnp.full(dist.shape, table(lo), F32)
    for b in range(lo + 1, hi + 1):
        out = jnp.where(dist >= BUCKET_STARTS[b], table(b), out)
    return out


def _bias_kernel(rel_ref, moba_ref, swa_ref):
    h = pl.program_id(0)
    kk = lax.broadcasted_iota(jnp.int32, (T, T), 0)
    qq = lax.broadcasted_iota(jnp.int32, (T, T), 1)
    rel = qq - kk

    def lookup(dist, col, dist_lo, dist_hi):
        return _bias_of_distance(dist, lambda b: rel_ref[b, col] * LOG2E, dist_lo, dist_hi)

    for d in range(N_NEAR):
        dist = rel + d * T
        bias = lookup(dist, SWA_Q_HEADS + h, max(d * T - (T - 1), 0), d * T + (T - 1))
        if d == 0:
            bias = jnp.where(dist >= 0, bias, NEG)
        moba_ref[0, d] = bias
    far = jnp.full((T, T), N_NEAR * T, jnp.int32)
    moba_ref[0, N_NEAR] = lookup(far, SWA_Q_HEADS + h, N_NEAR * T, N_NEAR * T)

    for d in range(2):
        dist = rel + d * T
        bias = lookup(dist, h, 0, WINDOW - 1)
        valid = (dist >= 0) & (dist < WINDOW)
        swa_ref[0, d] = jnp.where(valid, bias, NEG)


def _bias_tables(rel_bias):
    return pl.pallas_call(
        _bias_kernel,
        grid=(MOBA_HEADS,),
        in_specs=[pl.BlockSpec(memory_space=pltpu.SMEM)],
        out_specs=[
            pl.BlockSpec((1, N_NEAR + 1, T, T), lambda h: (h, 0, 0, 0)),
            pl.BlockSpec((1, 2, T, T), lambda h: (h, 0, 0, 0)),
        ],
        out_shape=[
            jax.ShapeDtypeStruct((MOBA_HEADS, N_NEAR + 1, T, T), F32),
            jax.ShapeDtypeStruct((SWA_Q_HEADS, 2, T, T), F32),
        ],
        name="bias_tables",
    )(rel_bias)


def _inproj_kernel(x_ref, g_ref, wnn_ref, wt_ref,
                   ka_ref, ga_ref, kb_ref, gb_ref,
                   qta_ref, vta_ref, qtb_ref, vtb_ref):
    x = x_ref[0]
    r = lax.rsqrt(jnp.mean(x * x, axis=-1, keepdims=True) + RMS_EPS)
    h = ((x * r) * g_ref[...]).astype(BF16)

    tok = jnp.dot(h, wnn_ref[...], preferred_element_type=F32)
    o = 0
    for ref, width in ((ga_ref, SWA_WIDTH), (kb_ref, MOBA_WIDTH), (gb_ref, MOBA_WIDTH)):
        ref[0] = tok[:, o:o + width].astype(BF16)
        o += width

    feat = lax.dot_general(wt_ref[...], h, (((1,), (1,)), ((), ())),
                           preferred_element_type=F32)
    ka_ref[0] = feat[0:SWA_KV_WIDTH, :].T.astype(BF16)
    o = SWA_KV_WIDTH
    for ref, width in ((qta_ref, SWA_WIDTH), (vta_ref, SWA_KV_WIDTH),
                       (qtb_ref, MOBA_WIDTH), (vtb_ref, MOBA_WIDTH)):
        for c in range(ROW_TILE // T):
            ref[0, c] = feat[o:o + width, c * T:(c + 1) * T].astype(BF16)
        o += width


def _inproj(x, gain, w_nn, w_t):
    B, S, D = x.shape
    nt = S // ROW_TILE
    cpt = ROW_TILE // T
    n_nn = w_nn.shape[1]
    n_t = w_t.shape[0]

    def tok_spec(width):
        return pl.BlockSpec((1, ROW_TILE, width), lambda b, t: (b, t, 0))

    def feat_spec(width):
        return pl.BlockSpec((1, cpt, width, T), lambda b, t: (b, t, 0, 0))

    def tok_shape(width):
        return jax.ShapeDtypeStruct((B, S, width), BF16)

    def feat_shape(width):
        return jax.ShapeDtypeStruct((B, S // T, width, T), BF16)

    return pl.pallas_call(
        _inproj_kernel,
        grid=(B, nt),
        in_specs=[
            pl.BlockSpec((1, ROW_TILE, D), lambda b, t: (b, t, 0)),
            pl.BlockSpec((1, D), lambda b, t: (0, 0)),
            pl.BlockSpec((D, n_nn), lambda b, t: (0, 0)),
            pl.BlockSpec((n_t, D), lambda b, t: (0, 0)),
        ],
        out_specs=[tok_spec(SWA_KV_WIDTH), tok_spec(SWA_WIDTH), tok_spec(MOBA_WIDTH), tok_spec(MOBA_WIDTH),
                   feat_spec(SWA_WIDTH), feat_spec(SWA_KV_WIDTH), feat_spec(MOBA_WIDTH), feat_spec(MOBA_WIDTH)],
        out_shape=[tok_shape(SWA_KV_WIDTH), tok_shape(SWA_WIDTH), tok_shape(MOBA_WIDTH), tok_shape(MOBA_WIDTH),
                   feat_shape(SWA_WIDTH), feat_shape(SWA_KV_WIDTH), feat_shape(MOBA_WIDTH), feat_shape(MOBA_WIDTH)],
        compiler_params=pltpu.CompilerParams(
            dimension_semantics=("arbitrary", "arbitrary"), vmem_limit_bytes=VMEM_LIMIT),
        name="norm_inproj",
    )(x, gain, w_nn, w_t)


def _ones_pad_rows(width):
    row = lax.broadcasted_iota(jnp.int32, (V_ROWS - HEAD_DIM, width), 0)
    return jnp.where(row == 0, 1.0, 0.0).astype(BF16)


def _swa_kernel(sink_ref, qt_ref, k_ref, vt_ref, bias_ref, o_ref,
                cur0_ref, prev0_ref, cur1_ref, prev1_ref, *, per_step):
    bufs = ((cur0_ref, prev0_ref), (cur1_ref, prev1_ref))
    half = T // 2
    base = pl.program_id(1) * per_step
    zeros_q = jnp.zeros((HEAD_DIM, T), BF16)
    head_rows = [slice(h * HEAD_DIM, (h + 1) * HEAD_DIM) for h in range(SWA_Q_HEADS)]

    def score(slot, t):
        t = jnp.minimum(t, per_step - 1)
        blk = base + t
        row0 = pl.multiple_of(blk * T, T)
        k_cur = k_ref[0, pl.ds(row0, T), :]
        k_prev = k_ref[0, pl.ds(pl.multiple_of(jnp.maximum(row0 - half, 0), half), half), :]
        prev_pen = jnp.where(blk == 0, NEG, 0.0).astype(F32)
        maxes = []
        for head in range(SWA_Q_HEADS):
            parts = [zeros_q] * SWA_KV_HEADS
            parts[head // SWA_GROUP] = qt_ref[0, t, head_rows[head], :]
            q_pad = jnp.concatenate(parts, axis=0)
            s_cur = jnp.dot(k_cur, q_pad, preferred_element_type=F32) + bias_ref[head, 0]
            s_prev = (jnp.dot(k_prev, q_pad, preferred_element_type=F32)
                      + (bias_ref[head, 1, half:, :] + prev_pen))
            bufs[slot][0][head] = s_cur
            bufs[slot][1][head] = s_prev
            m = jnp.maximum(jnp.max(s_cur, axis=0, keepdims=True), jnp.max(s_prev, axis=0, keepdims=True))
            maxes.append(jnp.maximum(m, sink_ref[head] * LOG2E))
        return tuple(maxes)

    def finish(slot, t, maxes):
        blk = base + t
        blk_prev = jnp.maximum(blk - 1, 0)
        outs = []
        for g in range(SWA_KV_HEADS):
            v_cur = jnp.concatenate([vt_ref[0, blk, head_rows[g], :], _ones_pad_rows(T)], axis=0)
            v_prev = jnp.concatenate([vt_ref[0, blk_prev, head_rows[g], half:], _ones_pad_rows(half)], axis=0)
            for head in range(g * SWA_GROUP, (g + 1) * SWA_GROUP):
                m = maxes[head]
                p_cur = jnp.exp2(bufs[slot][0][head] - m).astype(BF16)
                p_prev = jnp.exp2(bufs[slot][1][head] - m).astype(BF16)
                acc = (jnp.dot(v_cur, p_cur, preferred_element_type=F32)
                       + jnp.dot(v_prev, p_prev, preferred_element_type=F32))
                den = acc[HEAD_DIM:HEAD_DIM + 1] + jnp.exp2(sink_ref[head] * LOG2E - m)
                outs.append(acc[:HEAD_DIM] / den)
        o_ref[0, pl.ds(pl.multiple_of(t * T, T), T), :] = jnp.concatenate(outs, axis=0).T.astype(BF16)

    def block_group(c, maxes):
        for n in range(SWA_UNROLL):
            t = c * SWA_UNROLL + n
            nxt = score((n + 1) % 2, t + 1)
            finish(n % 2, t, maxes)
            maxes = nxt
        return maxes

    lax.fori_loop(0, per_step // SWA_UNROLL, block_group, score(0, jnp.int32(0)))


def _swa(sinks, qt_a, k_a, vt_a, swa_bias):
    B, nblk = qt_a.shape[0], qt_a.shape[1]
    S = nblk * T
    per_step = nblk // SWA_SPLIT
    assert nblk % SWA_SPLIT == 0 and per_step % SWA_UNROLL == 0 and SWA_UNROLL % 2 == 0
    return pl.pallas_call(
        functools.partial(_swa_kernel, per_step=per_step),
        grid=(B, SWA_SPLIT),
        in_specs=[
            pl.BlockSpec(memory_space=pltpu.SMEM),
            pl.BlockSpec((1, per_step, SWA_WIDTH, T), lambda b, h: (b, h, 0, 0)),
            pl.BlockSpec((1, S, SWA_KV_WIDTH), lambda b, h: (b, 0, 0)),
            pl.BlockSpec((1, nblk, SWA_KV_WIDTH, T), lambda b, h: (b, 0, 0, 0)),
            pl.BlockSpec((SWA_Q_HEADS, 2, T, T), lambda b, h: (0, 0, 0, 0)),
        ],
        out_specs=pl.BlockSpec((1, per_step * T, SWA_WIDTH), lambda b, h: (b, h, 0)),
        out_shape=jax.ShapeDtypeStruct((B, S, SWA_WIDTH), BF16),
        scratch_shapes=[pltpu.VMEM((SWA_Q_HEADS, T, T), F32),
                        pltpu.VMEM((SWA_Q_HEADS, T // 2, T), F32)] * 2,
        compiler_params=pltpu.CompilerParams(
            dimension_semantics=("arbitrary", "arbitrary"), vmem_limit_bytes=VMEM_LIMIT),
        name="swa_attention",
    )(sinks, qt_a, k_a, vt_a, swa_bias)


def _moba_kernel(qt_ref, k_ref, vt_ref, bias_ref, o_ref,
                 kaug_ref, vaug_ref, kmean_ref, qaug_ref, acc_ref,
                 s00_ref, s01_ref, s10_ref, s11_ref, *, nblk):
    s_refs = ((s00_ref, s01_ref), (s10_ref, s11_ref))
    heads = range(2)
    head_rows = [slice(hl * HEAD_DIM, (hl + 1) * HEAD_DIM) for hl in heads]

    rows = KEY_GROUP * T
    sel_r = lax.broadcasted_iota(jnp.int32, (2 * HEAD_DIM, AUG), 0)
    sel_c = lax.broadcasted_iota(jnp.int32, (2 * HEAD_DIM, AUG), 1)
    picks = [jnp.where((sel_r == sel_c + hl * HEAD_DIM) & (sel_c < HEAD_DIM), 1.0, 0.0).astype(BF16)
             for hl in heads]
    lane = lax.broadcasted_iota(jnp.int32, (rows, AUG), 1)
    row_blk = lax.shift_right_logical(lax.broadcasted_iota(jnp.int32, (rows, AUG), 0), T.bit_length() - 1)
    pad = _ones_pad_rows(T)

    def prep(c, carry):
        row0 = pl.multiple_of(c * rows, rows)
        blk0 = pl.multiple_of(c * KEY_GROUP, KEY_GROUP)
        kb = k_ref[0, pl.ds(row0, rows), :]
        onehot = jnp.where(lane == HEAD_DIM + blk0 + row_blk, 1.0, 0.0)
        for hl in heads:
            kp = jnp.dot(kb, picks[hl], preferred_element_type=F32)
            kaug_ref[hl, pl.ds(row0, rows), :] = (kp + onehot).astype(BF16)
            kmean_ref[hl, pl.ds(blk0, KEY_GROUP), :] = (
                jnp.sum(kp.reshape(KEY_GROUP, T, AUG), axis=1) * (1.0 / T))
            for r in range(KEY_GROUP):
                vaug_ref[hl, c * (KEY_GROUP // 2) + r // 2, :, (r % 2) * T:(r % 2 + 1) * T] = (
                    jnp.concatenate([vt_ref[0, blk0 + r, head_rows[hl], :], pad], axis=0))
        return carry

    lax.fori_loop(0, nblk // KEY_GROUP, prep, 0)

    width = QUERY_GROUP * T
    blk_id = lax.broadcasted_iota(jnp.int32, (nblk, width), 0)
    lane_blk = lax.shift_right_logical(lax.broadcasted_iota(jnp.int32, (nblk, width), 1), T.bit_length() - 1)
    zeros_q = jnp.zeros((AUG - HEAD_DIM, T), BF16)
    zeros_tail = jnp.zeros((AUG - HEAD_DIM - nblk, T), BF16)
    kmeans = []
    for hl in heads:
        km = kmean_ref[hl]
        km_hi = km.astype(BF16)
        kmeans.append((km_hi, (km - km_hi.astype(F32)).astype(BF16)))

    def route(c, carry):
        blk0 = pl.multiple_of(c * QUERY_GROUP, QUERY_GROUP)
        q_blk = blk0 + lane_blk
        for hl in heads:
            km_hi, km_lo = kmeans[hl]
            qts = [qt_ref[0, blk0 + x, head_rows[hl], :] for x in range(QUERY_GROUP)]
            scs = []
            for qt in qts:
                qs = jnp.concatenate([qt, zeros_q], axis=0)
                scs.append(jnp.dot(km_hi, qs, preferred_element_type=F32)
                           + jnp.dot(km_lo, qs, preferred_element_type=F32))
            sc = jnp.concatenate(scs, axis=1)
            sc = jnp.where(blk_id < q_blk, sc, -jnp.inf)
            chosen = blk_id == q_blk
            for _ in range(MOBA_TOPK):
                mx = jnp.max(sc, axis=0, keepdims=True)
                first = jnp.min(jnp.where(sc == mx, blk_id, nblk), axis=0, keepdims=True)
                hit = (blk_id == first) & (mx > -jnp.inf)
                chosen = chosen | hit
                sc = jnp.where(hit, -jnp.inf, sc)
            sel_bias = jnp.where(chosen, 0.0, NEG).astype(BF16)
            for x in range(QUERY_GROUP):
                qaug_ref[hl, blk0 + x] = jnp.concatenate(
                    [qts[x], sel_bias[:, x * T:(x + 1) * T], zeros_tail], axis=0)
        return carry

    lax.fori_loop(0, nblk // QUERY_GROUP, route, 0)

    def next_item(i, u):
        last = u + 1 == (i + 2) // 2
        return jnp.where(last, i + 1, i), jnp.where(last, 0, u + 1)

    def score(hl, slot, i, u, m):
        i = jnp.minimum(i, nblk - 1)
        row0 = pl.multiple_of(u * 2 * T, 2 * T)
        st = jnp.dot(kaug_ref[hl, pl.ds(row0, 2 * T), :], qaug_ref[hl, i], preferred_element_type=F32)
        s_even = st[0:T] + bias_ref[hl, jnp.clip(i - 2 * u, 0, N_NEAR)]
        s_odd = st[T:] + bias_ref[hl, jnp.clip(i - 2 * u - 1, 0, N_NEAR)]
        s_refs[hl][slot][0:T, :] = s_even
        s_refs[hl][slot][T:, :] = s_odd
        m = jnp.where(u == 0, NEG, m)
        top = jnp.maximum(jnp.max(s_even, axis=0, keepdims=True), jnp.max(s_odd, axis=0, keepdims=True))
        m_new = jnp.maximum(m, top)
        return m_new, jnp.exp2(m - m_new)

    def accumulate(hl, slot, i, u, m, alpha):
        p = jnp.exp2(s_refs[hl][slot][...] - m).astype(BF16)
        acc_ref[hl, i] = acc_ref[hl, i] * alpha + jnp.dot(vaug_ref[hl, u], p, preferred_element_type=F32)

    def item_step(slot, i, u, state):
        i_nxt, u_nxt = next_item(i, u)
        nxt = tuple(score(hl, 1 - slot, i_nxt, u_nxt, state[hl][0]) for hl in heads)
        for hl in heads:
            accumulate(hl, slot, i, u, *state[hl])
        return i_nxt, u_nxt, nxt

    def item_group(_, carry):
        i, u, state = carry
        for n in range(ITEM_UNROLL):
            i, u, state = item_step(n % 2, i, u, state)
        return i, u, state

    acc_ref[...] = jnp.zeros(acc_ref.shape, F32)
    zero = jnp.int32(0)
    m0 = jnp.full((1, T), NEG, F32)
    state = tuple(score(hl, 0, zero, zero, m0) for hl in heads)
    n_items = sum((i + 2) // 2 for i in range(nblk))
    assert n_items % ITEM_UNROLL == 0 and ITEM_UNROLL % 2 == 0
    lax.fori_loop(0, n_items // ITEM_UNROLL, item_group, (zero, zero, state))

    def finish(c, carry):
        for x in range(QUERY_GROUP):
            i = c * QUERY_GROUP + x
            outs = []
            for hl in heads:
                acc = acc_ref[hl, i]
                outs.append(acc[:HEAD_DIM] / acc[HEAD_DIM:HEAD_DIM + 1])
            o_ref[0, pl.ds(pl.multiple_of(i * T, T), T), :] = jnp.concatenate(outs, axis=0).T.astype(BF16)
        return carry

    lax.fori_loop(0, nblk // QUERY_GROUP, finish, 0)


def _moba(qt_b, k_b, vt_b, moba_bias):
    B, nblk = qt_b.shape[0], qt_b.shape[1]
    S = nblk * T
    assert HEAD_DIM + nblk <= AUG, "block one-hot must fit the augmented contraction width"
    assert nblk % KEY_GROUP == 0 and nblk % QUERY_GROUP == 0 and nblk % 2 == 0
    pair = 2 * HEAD_DIM
    return pl.pallas_call(
        functools.partial(_moba_kernel, nblk=nblk),
        grid=(B, MOBA_HEADS // 2),
        in_specs=[
            pl.BlockSpec((1, nblk, pair, T), lambda b, h: (b, 0, h, 0)),
            pl.BlockSpec((1, S, pair), lambda b, h: (b, 0, h)),
            pl.BlockSpec((1, nblk, pair, T), lambda b, h: (b, 0, h, 0)),
            pl.BlockSpec((2, N_NEAR + 1, T, T), lambda b, h: (h, 0, 0, 0)),
        ],
        out_specs=pl.BlockSpec((1, S, pair), lambda b, h: (b, 0, h)),
        out_shape=jax.ShapeDtypeStruct((B, S, MOBA_WIDTH), BF16),
        scratch_shapes=[
            pltpu.VMEM((2, S, AUG), BF16),
            pltpu.VMEM((2, nblk // 2, V_ROWS, 2 * T), BF16),
            pltpu.VMEM((2, nblk, AUG), F32),
            pltpu.VMEM((2, nblk, AUG, T), BF16),
            pltpu.VMEM((2, nblk, V_ROWS, T), F32),
        ] + [pltpu.VMEM((2 * T, T), F32)] * 4,
        compiler_params=pltpu.CompilerParams(
            dimension_semantics=("arbitrary", "arbitrary"), vmem_limit_bytes=VMEM_LIMIT),
        name="moba_attention",
    )(qt_b, k_b, vt_b, moba_bias)


def _sigmoid(v):
    return 1.0 / (1.0 + jnp.exp(-v))


def _rms(v, gain):
    r = lax.rsqrt(jnp.mean(v * v, axis=-1, keepdims=True) + RMS_EPS)
    return (v * r) * gain


def _outproj_kernel(x_ref, ao_ref, bo_ref, ga_ref, gb_ref, p_ref,
                    wo_ref, pg_ref, wg_ref, wp_ref, fg_ref, o_ref, *, final):
    chunk = OUT_ROW_TILE // OUT_CHUNKS
    chunks = [slice(c * chunk, (c + 1) * chunk) for c in range(OUT_CHUNKS)]

    def mix(rows):
        ga = ga_ref[0, rows, :].astype(F32)
        gb = gb_ref[0, rows, :].astype(F32)
        ma = (ao_ref[0, rows, :].astype(F32) * (ga * _sigmoid(ga))).astype(BF16)
        mb = (bo_ref[0, rows, :].astype(F32) * (gb * _sigmoid(gb))).astype(BF16)
        return (x_ref[0, rows, :]
                + jnp.dot(ma, wo_ref[0:SWA_WIDTH, :], preferred_element_type=F32)
                + jnp.dot(mb, wo_ref[SWA_WIDTH:, :], preferred_element_type=F32))

    def gate_up(rows, x1):
        n = _rms(x1, pg_ref[...]).astype(BF16)
        gate = _sigmoid(jnp.dot(n, wg_ref[...], preferred_element_type=F32))
        up = jnp.dot(p_ref[0, rows, :].astype(BF16), wp_ref[...], preferred_element_type=F32)
        return x1 + up * gate

    x1s = [mix(rows) for rows in chunks]
    ys = [gate_up(rows, x1) for rows, x1 in zip(chunks, x1s)]
    for rows, y in zip(chunks, ys):
        o_ref[0, rows, :] = _rms(y, fg_ref[...]) if final else y


def _outproj(x, a_out, b_out, g_a, g_b, p, w_out, ple_gain, w_gate, w_proj, final_gain, final):
    B, S, D = x.shape
    nt = S // OUT_ROW_TILE
    ple = p.shape[-1]

    def tok(width):
        return pl.BlockSpec((1, OUT_ROW_TILE, width), lambda b, t: (b, t, 0))

    def whole(shape):
        return pl.BlockSpec(shape, lambda b, t: (0,) * len(shape))

    return pl.pallas_call(
        functools.partial(_outproj_kernel, final=final),
        grid=(B, nt),
        in_specs=[tok(D), tok(SWA_WIDTH), tok(MOBA_WIDTH), tok(SWA_WIDTH), tok(MOBA_WIDTH), tok(ple),
                  whole((D, D)), whole((1, D)), whole((D, D)), whole((ple, D)), whole((1, D))],
        out_specs=tok(D),
        out_shape=jax.ShapeDtypeStruct((B, S, D), F32),
        compiler_params=pltpu.CompilerParams(
            dimension_semantics=("arbitrary", "arbitrary"), vmem_limit_bytes=VMEM_LIMIT),
        name="outproj_ple",
    )(x, a_out, b_out, g_a, g_b, p, w_out, ple_gain, w_gate, w_proj, final_gain)


def _wprep_kernel(win_ref, wo_ref, wg_ref, wp_ref, wnn_ref, wt_ref, wob_ref, wgb_ref, wpb_ref):
    widths = (SWA_WIDTH, SWA_KV_WIDTH, SWA_KV_WIDTH, SWA_WIDTH,
              MOBA_WIDTH, MOBA_WIDTH, MOBA_WIDTH, MOBA_WIDTH)
    offs = [0]
    for wd in widths:
        offs.append(offs[-1] + wd)
    w = win_ref[...]
    a_q, a_k, a_v, a_g, b_q, b_k, b_v, b_g = (w[:, offs[n]:offs[n + 1]] for n in range(8))
    scale = HEAD_DIM ** -0.5 * LOG2E
    wnn_ref[...] = jnp.concatenate([a_g, b_k, b_g], axis=1).astype(BF16)
    wt_ref[...] = jnp.concatenate([a_k, a_q * scale, a_v, b_q * scale, b_v], axis=1).T.astype(BF16)
    wob_ref[...] = wo_ref[...].astype(BF16)
    wgb_ref[...] = wg_ref[...].astype(BF16)
    wpb_ref[...] = wp_ref[...].astype(BF16)


def _prepare_weights(w_in, w_out, w_gate, w_proj):
    D, n_in = w_in.shape
    ple = w_proj.shape[0]
    steps = D // W_ROWS
    n_nn = SWA_WIDTH + 2 * MOBA_WIDTH
    n_t = n_in - n_nn
    rows = lambda width: pl.BlockSpec((W_ROWS, width), lambda k: (k, 0))
    return pl.pallas_call(
        _wprep_kernel,
        grid=(steps,),
        in_specs=[rows(n_in), rows(D), rows(D), pl.BlockSpec((ple // steps, D), lambda k: (k, 0))],
        out_specs=[rows(n_nn), pl.BlockSpec((n_t, W_ROWS), lambda k: (0, k)), rows(D), rows(D),
                   pl.BlockSpec((ple // steps, D), lambda k: (k, 0))],
        out_shape=[jax.ShapeDtypeStruct((D, n_nn), BF16), jax.ShapeDtypeStruct((n_t, D), BF16),
                   jax.ShapeDtypeStruct((D, D), BF16), jax.ShapeDtypeStruct((D, D), BF16),
                   jax.ShapeDtypeStruct((ple, D), BF16)],
        name="weight_prep",
    )(w_in, w_out, w_gate, w_proj)


def kernel(x, p, norm_in, w_in, sinks, rel_bias, w_out, ple_norm, w_ple_gate, w_ple_proj, final_norm):
    B, S, D = x.shape
    depth = p.shape[0]
    assert S % ROW_TILE == 0 and ROW_TILE % T == 0 and S % OUT_ROW_TILE == 0
    moba_bias, swa_bias = _bias_tables(rel_bias)
    for i in range(depth):
        w_nn, w_t, w_o, w_g, w_p = _prepare_weights(w_in[i], w_out[i], w_ple_gate[i], w_ple_proj[i])
        k_a, g_a, k_b, g_b, qt_a, vt_a, qt_b, vt_b = _inproj(x, norm_in[i][None, :], w_nn, w_t)
        a_out = _swa(sinks[i], qt_a, k_a, vt_a, swa_bias)
        b_out = _moba(qt_b, k_b, vt_b, moba_bias)
        x = _outproj(x, a_out, b_out, g_a, g_b, p[i], w_o, ple_norm[i][None, :], w_g, w_p,
                     final_norm[None, :], final=(i == depth - 1))
    return x
```

```python
import functools
import math

import jax
import jax.numpy as jnp
from jax import lax
from jax.experimental import pallas as pl
from jax.experimental.pallas import tpu as pltpu

F32 = jnp.float32
BF16 = jnp.bfloat16

HEAD_DIM = 64
SWA_Q_HEADS = 8
SWA_KV_HEADS = 2
MOBA_HEADS = 8
WINDOW = 128
MOBA_BLOCK = 256
MOBA_TOPK = 3
NUM_BUCKETS = 32
MAX_DISTANCE = 1024
RMS_EPS = 1e-6

SWA_WIDTH = SWA_Q_HEADS * HEAD_DIM
SWA_KV_WIDTH = SWA_KV_HEADS * HEAD_DIM
MOBA_WIDTH = MOBA_HEADS * HEAD_DIM
SWA_GROUP = SWA_Q_HEADS // SWA_KV_HEADS

T = MOBA_BLOCK
AUG = 128
V_ROWS = HEAD_DIM + 16
NEG = -1e30
LOG2E = math.log2(math.e)
N_NEAR = -(-(MAX_DISTANCE + T) // T)
KEY_GROUP = 8
QUERY_GROUP = 4
ITEM_UNROLL = 34
assert WINDOW <= T // 2
SWA_SPLIT = 2
SWA_UNROLL = 8
ROW_TILE = 512
W_ROWS = 128
OUT_ROW_TILE = 1024
OUT_CHUNKS = 8
VMEM_LIMIT = 56 * 1024 * 1024


def _bucket_starts():
    max_exact = NUM_BUCKETS // 2
    starts = list(range(max_exact))
    for j in range(NUM_BUCKETS - max_exact):
        edge = max_exact * (MAX_DISTANCE / max_exact) ** (j / (NUM_BUCKETS - max_exact))
        starts.append(math.ceil(edge - 1e-9))
    return starts


BUCKET_STARTS = _bucket_starts()


def _bias_of_distance(dist, table, dist_lo, dist_hi):
    bucket = lambda n: max(b for b in range(NUM_BUCKETS) if BUCKET_STARTS[b] <= max(n, 0))
    lo, hi = bucket(dist_lo), bucket(dist_hi)
    out = jnp.full(dist.shape, table(lo), F32)
    for b in range(lo + 1, hi + 1):
        out = jnp.where(dist >= BUCKET_STARTS[b], table(b), out)
    return out


def _bias_kernel(rel_ref, moba_ref, swa_ref):
    h = pl.program_id(0)
    kk = lax.broadcasted_iota(jnp.int32, (T, T), 0)
    qq = lax.broadcasted_iota(jnp.int32, (T, T), 1)
    rel = qq - kk

    def lookup(dist, col, dist_lo, dist_hi):
        return _bias_of_distance(dist, lambda b: rel_ref[b, col] * LOG2E, dist_lo, dist_hi)

    for d in range(N_NEAR):
        dist = rel + d * T
        bias = lookup(dist, SWA_Q_HEADS + h, max(d * T - (T - 1), 0), d * T + (T - 1))
        if d == 0:
            bias = jnp.where(dist >= 0, bias, NEG)
        moba_ref[0, d] = bias
    far = jnp.full((T, T), N_NEAR * T, jnp.int32)
    moba_ref[0, N_NEAR] = lookup(far, SWA_Q_HEADS + h, N_NEAR * T, N_NEAR * T)

    for d in range(2):
        dist = rel + d * T
        bias = lookup(dist, h, 0, WINDOW - 1)
        valid = (dist >= 0) & (dist < WINDOW)
        swa_ref[0, d] = jnp.where(valid, bias, NEG)


def _bias_tables(rel_bias):
    return pl.pallas_call(
        _bias_kernel,
        grid=(MOBA_HEADS,),
        in_specs=[pl.BlockSpec(memory_space=pltpu.SMEM)],
        out_specs=[
            pl.BlockSpec((1, N_NEAR + 1, T, T), lambda h: (h, 0, 0, 0)),
            pl.BlockSpec((1, 2, T, T), lambda h: (h, 0, 0, 0)),
        ],
        out_shape=[
            jax.ShapeDtypeStruct((MOBA_HEADS, N_NEAR + 1, T, T), F32),
            jax.ShapeDtypeStruct((SWA_Q_HEADS, 2, T, T), F32),
        ],
        name="bias_tables",
    )(rel_bias)


def _inproj_kernel(x_ref, g_ref, wnn_ref, wt_ref,
                   ka_ref, ga_ref, kb_ref, gb_ref,
                   qta_ref, vta_ref, qtb_ref, vtb_ref):
    x = x_ref[0]
    r = lax.rsqrt(jnp.mean(x * x, axis=-1, keepdims=True) + RMS_EPS)
    h = ((x * r) * g_ref[...]).astype(BF16)

    tok = jnp.dot(h, wnn_ref[...], preferred_element_type=F32)
    o = 0
    for ref, width in ((ga_ref, SWA_WIDTH), (kb_ref, MOBA_WIDTH), (gb_ref, MOBA_WIDTH)):
        ref[0] = tok[:, o:o + width].astype(BF16)
        o += width

    feat = lax.dot_general(wt_ref[...], h, (((1,), (1,)), ((), ())),
                           preferred_element_type=F32)
    ka_ref[0] = feat[0:SWA_KV_WIDTH, :].T.astype(BF16)
    o = SWA_KV_WIDTH
    for ref, width in ((qta_ref, SWA_WIDTH), (vta_ref, SWA_KV_WIDTH),
                       (qtb_ref, MOBA_WIDTH), (vtb_ref, MOBA_WIDTH)):
        for c in range(ROW_TILE // T):
            ref[0, c] = feat[o:o + width, c * T:(c + 1) * T].astype(BF16)
        o += width


def _inproj(x, gain, w_nn, w_t):
    B, S, D = x.shape
    nt = S // ROW_TILE
    cpt = ROW_TILE // T
    n_nn = w_nn.shape[1]
    n_t = w_t.shape[0]

    def tok_spec(width):
        return pl.BlockSpec((1, ROW_TILE, width), lambda b, t: (b, t, 0))

    def feat_spec(width):
        return pl.BlockSpec((1, cpt, width, T), lambda b, t: (b, t, 0, 0))

    def tok_shape(width):
        return jax.ShapeDtypeStruct((B, S, width), BF16)

    def feat_shape(width):
        return jax.ShapeDtypeStruct((B, S // T, width, T), BF16)

    return pl.pallas_call(
        _inproj_kernel,
        grid=(B, nt),
        in_specs=[
            pl.BlockSpec((1, ROW_TILE, D), lambda b, t: (b, t, 0)),
            pl.BlockSpec((1, D), lambda b, t: (0, 0)),
            pl.BlockSpec((D, n_nn), lambda b, t: (0, 0)),
            pl.BlockSpec((n_t, D), lambda b, t: (0, 0)),
        ],
        out_specs=[tok_spec(SWA_KV_WIDTH), tok_spec(SWA_WIDTH), tok_spec(MOBA_WIDTH), tok_spec(MOBA_WIDTH),
                   feat_spec(SWA_WIDTH), feat_spec(SWA_KV_WIDTH), feat_spec(MOBA_WIDTH), feat_spec(MOBA_WIDTH)],
        out_shape=[tok_shape(SWA_KV_WIDTH), tok_shape(SWA_WIDTH), tok_shape(MOBA_WIDTH), tok_shape(MOBA_WIDTH),
                   feat_shape(SWA_WIDTH), feat_shape(SWA_KV_WIDTH), feat_shape(MOBA_WIDTH), feat_shape(MOBA_WIDTH)],
        compiler_params=pltpu.CompilerParams(
            dimension_semantics=("arbitrary", "arbitrary"), vmem_limit_bytes=VMEM_LIMIT),
        name="norm_inproj",
    )(x, gain, w_nn, w_t)


def _ones_pad_rows(width):
    row = lax.broadcasted_iota(jnp.int32, (V_ROWS - HEAD_DIM, width), 0)
    return jnp.where(row == 0, 1.0, 0.0).astype(BF16)


def _swa_kernel(sink_ref, qt_ref, k_ref, vt_ref, bias_ref, o_ref,
                cur0_ref, prev0_ref, cur1_ref, prev1_ref, *, per_step):
    bufs = ((cur0_ref, prev0_ref), (cur1_ref, prev1_ref))
    half = T // 2
    base = pl.program_id(1) * per_step
    zeros_q = jnp.zeros((HEAD_DIM, T), BF16)
    head_rows = [slice(h * HEAD_DIM, (h + 1) * HEAD_DIM) for h in range(SWA_Q_HEADS)]

    def score(slot, t):
        t = jnp.minimum(t, per_step - 1)
        blk = base + t
        row0 = pl.multiple_of(blk * T, T)
        k_cur = k_ref[0, pl.ds(row0, T), :]
        k_prev = k_ref[0, pl.ds(pl.multiple_of(jnp.maximum(row0 - half, 0), half), half), :]
        prev_pen = jnp.where(blk == 0, NEG, 0.0).astype(F32)
        maxes = []
        for head in range(SWA_Q_HEADS):
            parts = [zeros_q] * SWA_KV_HEADS
            parts[head // SWA_GROUP] = qt_ref[0, t, head_rows[head], :]
            q_pad = jnp.concatenate(parts, axis=0)
            s_cur = jnp.dot(k_cur, q_pad, preferred_element_type=F32) + bias_ref[head, 0]
            s_prev = (jnp.dot(k_prev, q_pad, preferred_element_type=F32)
                      + (bias_ref[head, 1, half:, :] + prev_pen))
            bufs[slot][0][head] = s_cur
            bufs[slot][1][head] = s_prev
            m = jnp.maximum(jnp.max(s_cur, axis=0, keepdims=True), jnp.max(s_prev, axis=0, keepdims=True))
            maxes.append(jnp.maximum(m, sink_ref[head] * LOG2E))
        return tuple(maxes)

    def finish(slot, t, maxes):
        blk = base + t
        blk_prev = jnp.maximum(blk - 1, 0)
        outs = []
        for g in range(SWA_KV_HEADS):
            v_cur = jnp.concatenate([vt_ref[0, blk, head_rows[g], :], _ones_pad_rows(T)], axis=0)
            v_prev = jnp.concatenate([vt_ref[0, blk_prev, head_rows[g], half:], _ones_pad_rows(half)], axis=0)
            for head in range(g * SWA_GROUP, (g + 1) * SWA_GROUP):
                m = maxes[head]
                p_cur = jnp.exp2(bufs[slot][0][head] - m).astype(BF16)
                p_prev = jnp.exp2(bufs[slot][1][head] - m).astype(BF16)
                acc = (jnp.dot(v_cur, p_cur, preferred_element_type=F32)
                       + jnp.dot(v_prev, p_prev, preferred_element_type=F32))
                den = acc[HEAD_DIM:HEAD_DIM + 1] + jnp.exp2(sink_ref[head] * LOG2E - m)
                outs.append(acc[:HEAD_DIM] / den)
        o_ref[0, pl.ds(pl.multiple_of(t * T, T), T), :] = jnp.concatenate(outs, axis=0).T.astype(BF16)

    def block_group(c, maxes):
        for n in range(SWA_UNROLL):
            t = c * SWA_UNROLL + n
            nxt = score((n + 1) % 2, t + 1)
            finish(n % 2, t, maxes)
            maxes = nxt
        return maxes

    lax.fori_loop(0, per_step // SWA_UNROLL, block_group, score(0, jnp.int32(0)))


def _swa(sinks, qt_a, k_a, vt_a, swa_bias):
    B, nblk = qt_a.shape[0], qt_a.shape[1]
    S = nblk * T
    per_step = nblk // SWA_SPLIT
    assert nblk % SWA_SPLIT == 0 and per_step % SWA_UNROLL == 0 and SWA_UNROLL % 2 == 0
    return pl.pallas_call(
        functools.partial(_swa_kernel, per_step=per_step),
        grid=(B, SWA_SPLIT),
        in_specs=[
            pl.BlockSpec(memory_space=pltpu.SMEM),
            pl.BlockSpec((1, per_step, SWA_WIDTH, T), lambda b, h: (b, h, 0, 0)),
            pl.BlockSpec((1, S, SWA_KV_WIDTH), lambda b, h: (b, 0, 0)),
            pl.BlockSpec((1, nblk, SWA_KV_WIDTH, T), lambda b, h: (b, 0, 0, 0)),
            pl.BlockSpec((SWA_Q_HEADS, 2, T, T), lambda b, h: (0, 0, 0, 0)),
        ],
        out_specs=pl.BlockSpec((1, per_step * T, SWA_WIDTH), lambda b, h: (b, h, 0)),
        out_shape=jax.ShapeDtypeStruct((B, S, SWA_WIDTH), BF16),
        scratch_shapes=[pltpu.VMEM((SWA_Q_HEADS, T, T), F32),
                        pltpu.VMEM((SWA_Q_HEADS, T // 2, T), F32)] * 2,
        compiler_params=pltpu.CompilerParams(
            dimension_semantics=("arbitrary", "arbitrary"), vmem_limit_bytes=VMEM_LIMIT),
        name="swa_attention",
    )(sinks, qt_a, k_a, vt_a, swa_bias)


def _moba_kernel(qt_ref, k_ref, vt_ref, bias_ref, o_ref,
                 kaug_ref, vaug_ref, kmean_ref, qaug_ref, acc_ref,
                 s00_ref, s01_ref, s10_ref, s11_ref, *, nblk):
    s_refs = ((s00_ref, s01_ref), (s10_ref, s11_ref))
    heads = range(2)
    head_rows = [slice(hl * HEAD_DIM, (hl + 1) * HEAD_DIM) for hl in heads]

    rows = KEY_GROUP * T
    sel_r = lax.broadcasted_iota(jnp.int32, (2 * HEAD_DIM, AUG), 0)
    sel_c = lax.broadcasted_iota(jnp.int32, (2 * HEAD_DIM, AUG), 1)
    picks = [jnp.where((sel_r == sel_c + hl * HEAD_DIM) & (sel_c < HEAD_DIM), 1.0, 0.0).astype(BF16)
             for hl in heads]
    lane = lax.broadcasted_iota(jnp.int32, (rows, AUG), 1)
    row_blk = lax.shift_right_logical(lax.broadcasted_iota(jnp.int32, (rows, AUG), 0), T.bit_length() - 1)
    pad = _ones_pad_rows(T)

    def prep(c, carry):
        row0 = pl.multiple_of(c * rows, rows)
        blk0 = pl.multiple_of(c * KEY_GROUP, KEY_GROUP)
        kb = k_ref[0, pl.ds(row0, rows), :]
        onehot = jnp.where(lane == HEAD_DIM + blk0 + row_blk, 1.0, 0.0)
        for hl in heads:
            kp = jnp.dot(kb, picks[hl], preferred_element_type=F32)
            kaug_ref[hl, pl.ds(row0, rows), :] = (kp + onehot).astype(BF16)
            kmean_ref[hl, pl.ds(blk0, KEY_GROUP), :] = (
                jnp.sum(kp.reshape(KEY_GROUP, T, AUG), axis=1) * (1.0 / T))
            for r in range(KEY_GROUP):
                vaug_ref[hl, c * (KEY_GROUP // 2) + r // 2, :, (r % 2) * T:(r % 2 + 1) * T] = (
                    jnp.concatenate([vt_ref[0, blk0 + r, head_rows[hl], :], pad], axis=0))
        return carry

    lax.fori_loop(0, nblk // KEY_GROUP, prep, 0)

    width = QUERY_GROUP * T
    zeros_q = jnp.zeros((AUG - HEAD_DIM, T), BF16)
    kmeans = []
    for hl in heads:
        km = kmean_ref[hl]
        km_hi = km.astype(BF16)
        kmeans.append((km_hi, (km - km_hi.astype(F32)).astype(BF16)))

    for blk0 in range(0, nblk, QUERY_GROUP):
        rows_used = min(-(-(blk0 + QUERY_GROUP) // 16) * 16, nblk)
        blk_id = lax.broadcasted_iota(jnp.int32, (rows_used, width), 0)
        q_blk = blk0 + lax.shift_right_logical(
            lax.broadcasted_iota(jnp.int32, (rows_used, width), 1), T.bit_length() - 1)
        zeros_tail = jnp.zeros((AUG - HEAD_DIM - rows_used, T), BF16)
        for hl in heads:
            km_hi, km_lo = (km[:rows_used] for km in kmeans[hl])
            qts = [qt_ref[0, blk0 + x, head_rows[hl], :] for x in range(QUERY_GROUP)]
            scs = []
            for qt in qts:
                qs = jnp.concatenate([qt, zeros_q], axis=0)
                scs.append(jnp.dot(km_hi, qs, preferred_element_type=F32)
                           + jnp.dot(km_lo, qs, preferred_element_type=F32))
            sc = jnp.concatenate(scs, axis=1)
            sc = jnp.where(blk_id < q_blk, sc, -jnp.inf)
            chosen = blk_id == q_blk
            for _ in range(MOBA_TOPK):
                mx = jnp.max(sc, axis=0, keepdims=True)
                first = jnp.min(jnp.where(sc == mx, blk_id, nblk), axis=0, keepdims=True)
                hit = (blk_id == first) & (mx > -jnp.inf)
                chosen = chosen | hit
                sc = jnp.where(hit, -jnp.inf, sc)
            sel_bias = jnp.where(chosen, 0.0, NEG).astype(BF16)
            for x in range(QUERY_GROUP):
                qaug_ref[hl, blk0 + x] = jnp.concatenate(
                    [qts[x], sel_bias[:, x * T:(x + 1) * T], zeros_tail], axis=0)

    def next_item(i, u):
        last = u + 1 == (i + 2) // 2
        return jnp.where(last, i + 1, i), jnp.where(last, 0, u + 1)

    def score(hl, slot, i, u, m):
        i = jnp.minimum(i, nblk - 1)
        row0 = pl.multiple_of(u * 2 * T, 2 * T)
        st = jnp.dot(kaug_ref[hl, pl.ds(row0, 2 * T), :], qaug_ref[hl, i], preferred_element_type=F32)
        s_even = st[0:T] + bias_ref[hl, jnp.clip(i - 2 * u, 0, N_NEAR)]
        s_odd = st[T:] + bias_ref[hl, jnp.clip(i - 2 * u - 1, 0, N_NEAR)]
        s_refs[hl][slot][0:T, :] = s_even
        s_refs[hl][slot][T:, :] = s_odd
        m = jnp.where(u == 0, NEG, m)
        top = jnp.maximum(jnp.max(s_even, axis=0, keepdims=True), jnp.max(s_odd, axis=0, keepdims=True))
        m_new = jnp.maximum(m, top)
        return m_new, jnp.exp2(m - m_new)

    def accumulate(hl, slot, i, u, m, alpha):
        p = jnp.exp2(s_refs[hl][slot][...] - m).astype(BF16)
        acc_ref[hl, i] = acc_ref[hl, i] * alpha + jnp.dot(vaug_ref[hl, u], p, preferred_element_type=F32)

    def item_step(slot, i, u, state):
        i_nxt, u_nxt = next_item(i, u)
        nxt = tuple(score(hl, 1 - slot, i_nxt, u_nxt, state[hl][0]) for hl in heads)
        for hl in heads:
            accumulate(hl, slot, i, u, *state[hl])
        return i_nxt, u_nxt, nxt

    def item_group(_, carry):
        i, u, state = carry
        for n in range(ITEM_UNROLL):
            i, u, state = item_step(n % 2, i, u, state)
        return i, u, state

    acc_ref[...] = jnp.zeros(acc_ref.shape, F32)
    zero = jnp.int32(0)
    m0 = jnp.full((1, T), NEG, F32)
    state = tuple(score(hl, 0, zero, zero, m0) for hl in heads)
    n_items = sum((i + 2) // 2 for i in range(nblk))
    assert n_items % ITEM_UNROLL == 0 and ITEM_UNROLL % 2 == 0
    lax.fori_loop(0, n_items // ITEM_UNROLL, item_group, (zero, zero, state))

    def finish(c, carry):
        for x in range(QUERY_GROUP):
            i = c * QUERY_GROUP + x
            outs = []
            for hl in heads:
                acc = acc_ref[hl, i]
                outs.append(acc[:HEAD_DIM] / acc[HEAD_DIM:HEAD_DIM + 1])
            o_ref[0, pl.ds(pl.multiple_of(i * T, T), T), :] = jnp.concatenate(outs, axis=0).T.astype(BF16)
        return carry

    lax.fori_loop(0, nblk // QUERY_GROUP, finish, 0)


def _moba(qt_b, k_b, vt_b, moba_bias):
    B, nblk = qt_b.shape[0], qt_b.shape[1]
    S = nblk * T
    assert HEAD_DIM + nblk <= AUG, "block one-hot must fit the augmented contraction width"
    assert nblk % KEY_GROUP == 0 and nblk % QUERY_GROUP == 0 and nblk % 2 == 0
    pair = 2 * HEAD_DIM
    return pl.pallas_call(
        functools.partial(_moba_kernel, nblk=nblk),
        grid=(B, MOBA_HEADS // 2),
        in_specs=[
            pl.BlockSpec((1, nblk, pair, T), lambda b, h: (b, 0, h, 0)),
            pl.BlockSpec((1, S, pair), lambda b, h: (b, 0, h)),
            pl.BlockSpec((1, nblk, pair, T), lambda b, h: (b, 0, h, 0)),
            pl.BlockSpec((2, N_NEAR + 1, T, T), lambda b, h: (h, 0, 0, 0)),
        ],
        out_specs=pl.BlockSpec((1, S, pair), lambda b, h: (b, 0, h)),
        out_shape=jax.ShapeDtypeStruct((B, S, MOBA_WIDTH), BF16),
        scratch_shapes=[
            pltpu.VMEM((2, S, AUG), BF16),
            pltpu.VMEM((2, nblk // 2, V_ROWS, 2 * T), BF16),
            pltpu.VMEM((2, nblk, AUG), F32),
            pltpu.VMEM((2, nblk, AUG, T), BF16),
            pltpu.VMEM((2, nblk, V_ROWS, T), F32),
        ] + [pltpu.VMEM((2 * T, T), F32)] * 4,
        compiler_params=pltpu.CompilerParams(
            dimension_semantics=("arbitrary", "arbitrary"), vmem_limit_bytes=VMEM_LIMIT),
        name="moba_attention",
    )(qt_b, k_b, vt_b, moba_bias)


def _sigmoid(v):
    return 1.0 / (1.0 + jnp.exp(-v))


def _rms(v, gain):
    r = lax.rsqrt(jnp.mean(v * v, axis=-1, keepdims=True) + RMS_EPS)
    return (v * r) * gain


def _outproj_kernel(x_ref, ao_ref, bo_ref, ga_ref, gb_ref, p_ref,
                    wo_ref, pg_ref, wg_ref, wp_ref, fg_ref, o_ref, *, final):
    chunk = OUT_ROW_TILE // OUT_CHUNKS
    chunks = [slice(c * chunk, (c + 1) * chunk) for c in range(OUT_CHUNKS)]

    def mix(rows):
        ga = ga_ref[0, rows, :].astype(F32)
        gb = gb_ref[0, rows, :].astype(F32)
        ma = (ao_ref[0, rows, :].astype(F32) * (ga * _sigmoid(ga))).astype(BF16)
        mb = (bo_ref[0, rows, :].astype(F32) * (gb * _sigmoid(gb))).astype(BF16)
        return (x_ref[0, rows, :]
                + jnp.dot(ma, wo_ref[0:SWA_WIDTH, :], preferred_element_type=F32)
                + jnp.dot(mb, wo_ref[SWA_WIDTH:, :], preferred_element_type=F32))

    def gate_up(rows, x1):
        n = _rms(x1, pg_ref[...]).astype(BF16)
        gate = _sigmoid(jnp.dot(n, wg_ref[...], preferred_element_type=F32))
        up = jnp.dot(p_ref[0, rows, :].astype(BF16), wp_ref[...], preferred_element_type=F32)
        return x1 + up * gate

    x1s = [mix(rows) for rows in chunks]
    ys = [gate_up(rows, x1) for rows, x1 in zip(chunks, x1s)]
    for rows, y in zip(chunks, ys):
        o_ref[0, rows, :] = _rms(y, fg_ref[...]) if final else y


def _outproj(x, a_out, b_out, g_a, g_b, p, w_out, ple_gain, w_gate, w_proj, final_gain, final):
    B, S, D = x.shape
    nt = S // OUT_ROW_TILE
    ple = p.shape[-1]

    def tok(width):
        return pl.BlockSpec((1, OUT_ROW_TILE, width), lambda b, t: (b, t, 0))

    def whole(shape):
        return pl.BlockSpec(shape, lambda b, t: (0,) * len(shape))

    return pl.pallas_call(
        functools.partial(_outproj_kernel, final=final),
        grid=(B, nt),
        in_specs=[tok(D), tok(SWA_WIDTH), tok(MOBA_WIDTH), tok(SWA_WIDTH), tok(MOBA_WIDTH), tok(ple),
                  whole((D, D)), whole((1, D)), whole((D, D)), whole((ple, D)), whole((1, D))],
        out_specs=tok(D),
        out_shape=jax.ShapeDtypeStruct((B, S, D), F32),
        compiler_params=pltpu.CompilerParams(
            dimension_semantics=("arbitrary", "arbitrary"), vmem_limit_bytes=VMEM_LIMIT),
        name="outproj_ple",
    )(x, a_out, b_out, g_a, g_b, p, w_out, ple_gain, w_gate, w_proj, final_gain)


def _wprep_kernel(win_ref, wo_ref, wg_ref, wp_ref, wnn_ref, wt_ref, wob_ref, wgb_ref, wpb_ref):
    widths = (SWA_WIDTH, SWA_KV_WIDTH, SWA_KV_WIDTH, SWA_WIDTH,
              MOBA_WIDTH, MOBA_WIDTH, MOBA_WIDTH, MOBA_WIDTH)
    offs = [0]
    for wd in widths:
        offs.append(offs[-1] + wd)
    w = win_ref[...]
    a_q, a_k, a_v, a_g, b_q, b_k, b_v, b_g = (w[:, offs[n]:offs[n + 1]] for n in range(8))
    scale = HEAD_DIM ** -0.5 * LOG2E
    wnn_ref[...] = jnp.concatenate([a_g, b_k, b_g], axis=1).astype(BF16)
    wt_ref[...] = jnp.concatenate([a_k, a_q * scale, a_v, b_q * scale, b_v], axis=1).T.astype(BF16)
    wob_ref[...] = wo_ref[...].astype(BF16)
    wgb_ref[...] = wg_ref[...].astype(BF16)
    wpb_ref[...] = wp_ref[...].astype(BF16)


def _prepare_weights(w_in, w_out, w_gate, w_proj):
    D, n_in = w_in.shape
    ple = w_proj.shape[0]
    steps = D // W_ROWS
    n_nn = SWA_WIDTH + 2 * MOBA_WIDTH
    n_t = n_in - n_nn
    rows = lambda width: pl.BlockSpec((W_ROWS, width), lambda k: (k, 0))
    return pl.pallas_call(
        _wprep_kernel,
        grid=(steps,),
        in_specs=[rows(n_in), rows(D), rows(D), pl.BlockSpec((ple // steps, D), lambda k: (k, 0))],
        out_specs=[rows(n_nn), pl.BlockSpec((n_t, W_ROWS), lambda k: (0, k)), rows(D), rows(D),
                   pl.BlockSpec((ple // steps, D), lambda k: (k, 0))],
        out_shape=[jax.ShapeDtypeStruct((D, n_nn), BF16), jax.ShapeDtypeStruct((n_t, D), BF16),
                   jax.ShapeDtypeStruct((D, D), BF16), jax.ShapeDtypeStruct((D, D), BF16),
                   jax.ShapeDtypeStruct((ple, D), BF16)],
        name="weight_prep",
    )(w_in, w_out, w_gate, w_proj)


def kernel(x, p, norm_in, w_in, sinks, rel_bias, w_out, ple_norm, w_ple_gate, w_ple_proj, final_norm):
    B, S, D = x.shape
    depth = p.shape[0]
    assert S % ROW_TILE == 0 and ROW_TILE % T == 0 and S % OUT_ROW_TILE == 0
    moba_bias, swa_bias = _bias_tables(rel_bias)
    for i in range(depth):
        w_nn, w_t, w_o, w_g, w_p = _prepare_weights(w_in[i], w_out[i], w_ple_gate[i], w_ple_proj[i])
        k_a, g_a, k_b, g_b, qt_a, vt_a, qt_b, vt_b = _inproj(x, norm_in[i][None, :], w_nn, w_t)
        a_out = _swa(sinks[i], qt_a, k_a, vt_a, swa_bias)
        b_out = _moba(qt_b, k_b, vt_b, moba_bias)
        x = _outproj(x, a_out, b_out, g_a, g_b, p[i], w_o, ple_norm[i][None, :], w_g, w_p,
                     final_norm[None, :], final=(i == depth - 1))
    return x
```

```python
import functools
import math

import jax
import jax.numpy as jnp
from jax import lax
from jax.experimental import pallas as pl
from jax.experimental.pallas import tpu as pltpu

F32 = jnp.float32
BF16 = jnp.bfloat16

HEAD_DIM = 64
SWA_Q_HEADS = 8
SWA_KV_HEADS = 2
MOBA_HEADS = 8
WINDOW = 128
MOBA_BLOCK = 256
MOBA_TOPK = 3
NUM_BUCKETS = 32
MAX_DISTANCE = 1024
RMS_EPS = 1e-6

SWA_WIDTH = SWA_Q_HEADS * HEAD_DIM
SWA_KV_WIDTH = SWA_KV_HEADS * HEAD_DIM
MOBA_WIDTH = MOBA_HEADS * HEAD_DIM
SWA_GROUP = SWA_Q_HEADS // SWA_KV_HEADS

T = MOBA_BLOCK
AUG = 128
V_ROWS = HEAD_DIM + 16
NEG = -1e30
LOG2E = math.log2(math.e)
N_NEAR = -(-(MAX_DISTANCE + T) // T)
KEY_GROUP = 8
QUERY_GROUP = 4
ITEM_UNROLL = 34
assert WINDOW <= T // 2
SWA_SPLIT = 2
SWA_UNROLL = 8
ROW_TILE = 1024
W_ROWS = 128
OUT_ROW_TILE = 1024
OUT_CHUNKS = 8
VMEM_LIMIT = 56 * 1024 * 1024


def _bucket_starts():
    max_exact = NUM_BUCKETS // 2
    starts = list(range(max_exact))
    for j in range(NUM_BUCKETS - max_exact):
        edge = max_exact * (MAX_DISTANCE / max_exact) ** (j / (NUM_BUCKETS - max_exact))
        starts.append(math.ceil(edge - 1e-9))
    return starts


BUCKET_STARTS = _bucket_starts()


def _bias_of_distance(dist, table, dist_lo, dist_hi):
    bucket = lambda n: max(b for b in range(NUM_BUCKETS) if BUCKET_STARTS[b] <= max(n, 0))
    lo, hi = bucket(dist_lo), bucket(dist_hi)
    out = jnp.full(dist.shape, table(lo), F32)
    for b in range(lo + 1, hi + 1):
        out = jnp.where(dist >= BUCKET_STARTS[b], table(b), out)
    return out


def _bias_kernel(rel_ref, moba_ref, swa_ref):
    h = pl.program_id(0)
    kk = lax.broadcasted_iota(jnp.int32, (T, T), 0)
    qq = lax.broadcasted_iota(jnp.int32, (T, T), 1)
    rel = qq - kk

    def lookup(dist, col, dist_lo, dist_hi):
        return _bias_of_distance(dist, lambda b: rel_ref[b, col] * LOG2E, dist_lo, dist_hi)

    for d in range(N_NEAR):
        dist = rel + d * T
        bias = lookup(dist, SWA_Q_HEADS + h, max(d * T - (T - 1), 0), d * T + (T - 1))
        if d == 0:
            bias = jnp.where(dist >= 0, bias, NEG)
        moba_ref[0, d] = bias
    far = jnp.full((T, T), N_NEAR * T, jnp.int32)
    moba_ref[0, N_NEAR] = lookup(far, SWA_Q_HEADS + h, N_NEAR * T, N_NEAR * T)

    for d in range(2):
        dist = rel + d * T
        bias = lookup(dist, h, 0, WINDOW - 1)
        valid = (dist >= 0) & (dist < WINDOW)
        swa_ref[0, d] = jnp.where(valid, bias, NEG)


def _bias_tables(rel_bias):
    return pl.pallas_call(
        _bias_kernel,
        grid=(MOBA_HEADS,),
        in_specs=[pl.BlockSpec(memory_space=pltpu.SMEM)],
        out_specs=[
            pl.BlockSpec((1, N_NEAR + 1, T, T), lambda h: (h, 0, 0, 0)),
            pl.BlockSpec((1, 2, T, T), lambda h: (h, 0, 0, 0)),
        ],
        out_shape=[
            jax.ShapeDtypeStruct((MOBA_HEADS, N_NEAR + 1, T, T), F32),
            jax.ShapeDtypeStruct((SWA_Q_HEADS, 2, T, T), F32),
        ],
        name="bias_tables",
    )(rel_bias)


def _inproj_kernel(x_ref, g_ref, wnn_ref, wt_ref,
                   ka_ref, ga_ref, kb_ref, gb_ref,
                   qta_ref, vta_ref, qtb_ref, vtb_ref):
    x = x_ref[0]
    r = lax.rsqrt(jnp.mean(x * x, axis=-1, keepdims=True) + RMS_EPS)
    h = ((x * r) * g_ref[...]).astype(BF16)

    tok = jnp.dot(h, wnn_ref[...], preferred_element_type=F32)
    o = 0
    for ref, width in ((ga_ref, SWA_WIDTH), (kb_ref, MOBA_WIDTH), (gb_ref, MOBA_WIDTH)):
        ref[0] = tok[:, o:o + width].astype(BF16)
        o += width

    feat = lax.dot_general(wt_ref[...], h, (((1,), (1,)), ((), ())),
                           preferred_element_type=F32)
    ka_ref[0] = feat[0:SWA_KV_WIDTH, :].T.astype(BF16)
    o = SWA_KV_WIDTH
    for ref, width in ((qta_ref, SWA_WIDTH), (vta_ref, SWA_KV_WIDTH),
                       (qtb_ref, MOBA_WIDTH), (vtb_ref, MOBA_WIDTH)):
        for c in range(ROW_TILE // T):
            ref[0, c] = feat[o:o + width, c * T:(c + 1) * T].astype(BF16)
        o += width


def _inproj(x, gain, w_nn, w_t):
    B, S, D = x.shape
    nt = S // ROW_TILE
    cpt = ROW_TILE // T
    n_nn = w_nn.shape[1]
    n_t = w_t.shape[0]

    def tok_spec(width):
        return pl.BlockSpec((1, ROW_TILE, width), lambda b, t: (b, t, 0))

    def feat_spec(width):
        return pl.BlockSpec((1, cpt, width, T), lambda b, t: (b, t, 0, 0))

    def tok_shape(width):
        return jax.ShapeDtypeStruct((B, S, width), BF16)

    def feat_shape(width):
        return jax.ShapeDtypeStruct((B, S // T, width, T), BF16)

    return pl.pallas_call(
        _inproj_kernel,
        grid=(B, nt),
        in_specs=[
            pl.BlockSpec((1, ROW_TILE, D), lambda b, t: (b, t, 0)),
            pl.BlockSpec((1, D), lambda b, t: (0, 0)),
            pl.BlockSpec((D, n_nn), lambda b, t: (0, 0)),
            pl.BlockSpec((n_t, D), lambda b, t: (0, 0)),
        ],
        out_specs=[tok_spec(SWA_KV_WIDTH), tok_spec(SWA_WIDTH), tok_spec(MOBA_WIDTH), tok_spec(MOBA_WIDTH),
                   feat_spec(SWA_WIDTH), feat_spec(SWA_KV_WIDTH), feat_spec(MOBA_WIDTH), feat_spec(MOBA_WIDTH)],
        out_shape=[tok_shape(SWA_KV_WIDTH), tok_shape(SWA_WIDTH), tok_shape(MOBA_WIDTH), tok_shape(MOBA_WIDTH),
                   feat_shape(SWA_WIDTH), feat_shape(SWA_KV_WIDTH), feat_shape(MOBA_WIDTH), feat_shape(MOBA_WIDTH)],
        compiler_params=pltpu.CompilerParams(
            dimension_semantics=("arbitrary", "arbitrary"), vmem_limit_bytes=VMEM_LIMIT),
        name="norm_inproj",
    )(x, gain, w_nn, w_t)


def _ones_pad_rows(width):
    row = lax.broadcasted_iota(jnp.int32, (V_ROWS - HEAD_DIM, width), 0)
    return jnp.where(row == 0, 1.0, 0.0).astype(BF16)


def _swa_kernel(sink_ref, qt_ref, k_ref, vt_ref, bias_ref, o_ref,
                cur0_ref, prev0_ref, cur1_ref, prev1_ref, *, per_step):
    bufs = ((cur0_ref, prev0_ref), (cur1_ref, prev1_ref))
    half = T // 2
    base = pl.program_id(1) * per_step
    zeros_q = jnp.zeros((HEAD_DIM, T), BF16)
    head_rows = [slice(h * HEAD_DIM, (h + 1) * HEAD_DIM) for h in range(SWA_Q_HEADS)]

    def score(slot, t):
        t = jnp.minimum(t, per_step - 1)
        blk = base + t
        row0 = pl.multiple_of(blk * T, T)
        k_cur = k_ref[0, pl.ds(row0, T), :]
        k_prev = k_ref[0, pl.ds(pl.multiple_of(jnp.maximum(row0 - half, 0), half), half), :]
        prev_pen = jnp.where(blk == 0, NEG, 0.0).astype(F32)
        maxes = []
        for head in range(SWA_Q_HEADS):
            parts = [zeros_q] * SWA_KV_HEADS
            parts[head // SWA_GROUP] = qt_ref[0, t, head_rows[head], :]
            q_pad = jnp.concatenate(parts, axis=0)
            s_cur = jnp.dot(k_cur, q_pad, preferred_element_type=F32) + bias_ref[head, 0]
            s_prev = (jnp.dot(k_prev, q_pad, preferred_element_type=F32)
                      + (bias_ref[head, 1, half:, :] + prev_pen))
            bufs[slot][0][head] = s_cur
            bufs[slot][1][head] = s_prev
            m = jnp.maximum(jnp.max(s_cur, axis=0, keepdims=True), jnp.max(s_prev, axis=0, keepdims=True))
            maxes.append(jnp.maximum(m, sink_ref[head] * LOG2E))
        return tuple(maxes)

    def finish(slot, t, maxes):
        blk = base + t
        blk_prev = jnp.maximum(blk - 1, 0)
        outs = []
        for g in range(SWA_KV_HEADS):
            v_cur = jnp.concatenate([vt_ref[0, blk, head_rows[g], :], _ones_pad_rows(T)], axis=0)
            v_prev = jnp.concatenate([vt_ref[0, blk_prev, head_rows[g], half:], _ones_pad_rows(half)], axis=0)
            for head in range(g * SWA_GROUP, (g + 1) * SWA_GROUP):
                m = maxes[head]
                p_cur = jnp.exp2(bufs[slot][0][head] - m).astype(BF16)
                p_prev = jnp.exp2(bufs[slot][1][head] - m).astype(BF16)
                acc = (jnp.dot(v_cur, p_cur, preferred_element_type=F32)
                       + jnp.dot(v_prev, p_prev, preferred_element_type=F32))
                den = acc[HEAD_DIM:HEAD_DIM + 1] + jnp.exp2(sink_ref[head] * LOG2E - m)
                outs.append(acc[:HEAD_DIM] / den)
        o_ref[0, pl.ds(pl.multiple_of(t * T, T), T), :] = jnp.concatenate(outs, axis=0).T.astype(BF16)

    def block_group(c, maxes):
        for n in range(SWA_UNROLL):
            t = c * SWA_UNROLL + n
            nxt = score((n + 1) % 2, t + 1)
            finish(n % 2, t, maxes)
            maxes = nxt
        return maxes

    lax.fori_loop(0, per_step // SWA_UNROLL, block_group, score(0, jnp.int32(0)))


def _swa(sinks, qt_a, k_a, vt_a, swa_bias):
    B, nblk = qt_a.shape[0], qt_a.shape[1]
    S = nblk * T
    per_step = nblk // SWA_SPLIT
    assert nblk % SWA_SPLIT == 0 and per_step % SWA_UNROLL == 0 and SWA_UNROLL % 2 == 0
    return pl.pallas_call(
        functools.partial(_swa_kernel, per_step=per_step),
        grid=(B, SWA_SPLIT),
        in_specs=[
            pl.BlockSpec(memory_space=pltpu.SMEM),
            pl.BlockSpec((1, per_step, SWA_WIDTH, T), lambda b, h: (b, h, 0, 0)),
            pl.BlockSpec((1, S, SWA_KV_WIDTH), lambda b, h: (b, 0, 0)),
            pl.BlockSpec((1, nblk, SWA_KV_WIDTH, T), lambda b, h: (b, 0, 0, 0)),
            pl.BlockSpec((SWA_Q_HEADS, 2, T, T), lambda b, h: (0, 0, 0, 0)),
        ],
        out_specs=pl.BlockSpec((1, per_step * T, SWA_WIDTH), lambda b, h: (b, h, 0)),
        out_shape=jax.ShapeDtypeStruct((B, S, SWA_WIDTH), BF16),
        scratch_shapes=[pltpu.VMEM((SWA_Q_HEADS, T, T), F32),
                        pltpu.VMEM((SWA_Q_HEADS, T // 2, T), F32)] * 2,
        compiler_params=pltpu.CompilerParams(
            dimension_semantics=("arbitrary", "arbitrary"), vmem_limit_bytes=VMEM_LIMIT),
        name="swa_attention",
    )(sinks, qt_a, k_a, vt_a, swa_bias)


def _moba_kernel(qt_ref, k_ref, vt_ref, bias_ref, o_ref,
                 kaug_ref, vaug_ref, kmean_ref, qaug_ref, acc_ref,
                 s00_ref, s01_ref, s10_ref, s11_ref, *, nblk):
    s_refs = ((s00_ref, s01_ref), (s10_ref, s11_ref))
    heads = range(2)
    head_rows = [slice(hl * HEAD_DIM, (hl + 1) * HEAD_DIM) for hl in heads]

    rows = KEY_GROUP * T
    sel_r = lax.broadcasted_iota(jnp.int32, (2 * HEAD_DIM, AUG), 0)
    sel_c = lax.broadcasted_iota(jnp.int32, (2 * HEAD_DIM, AUG), 1)
    picks = [jnp.where((sel_r == sel_c + hl * HEAD_DIM) & (sel_c < HEAD_DIM), 1.0, 0.0).astype(BF16)
             for hl in heads]
    lane = lax.broadcasted_iota(jnp.int32, (rows, AUG), 1)
    row_blk = lax.shift_right_logical(lax.broadcasted_iota(jnp.int32, (rows, AUG), 0), T.bit_length() - 1)
    pad = _ones_pad_rows(T)

    def prep(c, carry):
        row0 = pl.multiple_of(c * rows, rows)
        blk0 = pl.multiple_of(c * KEY_GROUP, KEY_GROUP)
        kb = k_ref[0, pl.ds(row0, rows), :]
        onehot = jnp.where(lane == HEAD_DIM + blk0 + row_blk, 1.0, 0.0)
        for hl in heads:
            kp = jnp.dot(kb, picks[hl], preferred_element_type=F32)
            kaug_ref[hl, pl.ds(row0, rows), :] = (kp + onehot).astype(BF16)
            kmean_ref[hl, pl.ds(blk0, KEY_GROUP), :] = (
                jnp.sum(kp.reshape(KEY_GROUP, T, AUG), axis=1) * (1.0 / T))
            for r in range(KEY_GROUP):
                vaug_ref[hl, c * (KEY_GROUP // 2) + r // 2, :, (r % 2) * T:(r % 2 + 1) * T] = (
                    jnp.concatenate([vt_ref[0, blk0 + r, head_rows[hl], :], pad], axis=0))
        return carry

    lax.fori_loop(0, nblk // KEY_GROUP, prep, 0)

    width = QUERY_GROUP * T
    zeros_q = jnp.zeros((AUG - HEAD_DIM, T), BF16)
    kmeans = []
    for hl in heads:
        km = kmean_ref[hl]
        km_hi = km.astype(BF16)
        kmeans.append((km_hi, (km - km_hi.astype(F32)).astype(BF16)))

    for blk0 in range(0, nblk, QUERY_GROUP):
        rows_used = min(-(-(blk0 + QUERY_GROUP) // 16) * 16, nblk)
        blk_id = lax.broadcasted_iota(jnp.int32, (rows_used, width), 0)
        q_blk = blk0 + lax.shift_right_logical(
            lax.broadcasted_iota(jnp.int32, (rows_used, width), 1), T.bit_length() - 1)
        zeros_tail = jnp.zeros((AUG - HEAD_DIM - rows_used, T), BF16)
        for hl in heads:
            km_hi, km_lo = (km[:rows_used] for km in kmeans[hl])
            qts = [qt_ref[0, blk0 + x, head_rows[hl], :] for x in range(QUERY_GROUP)]
            scs = []
            for qt in qts:
                qs = jnp.concatenate([qt, zeros_q], axis=0)
                scs.append(jnp.dot(km_hi, qs, preferred_element_type=F32)
                           + jnp.dot(km_lo, qs, preferred_element_type=F32))
            sc = jnp.concatenate(scs, axis=1)
            sc = jnp.where(blk_id < q_blk, sc, -jnp.inf)
            chosen = blk_id == q_blk
            for _ in range(MOBA_TOPK):
                mx = jnp.max(sc, axis=0, keepdims=True)
                first = jnp.min(jnp.where(sc == mx, blk_id, nblk), axis=0, keepdims=True)
                hit = (blk_id == first) & (mx > -jnp.inf)
                chosen = chosen | hit
                sc = jnp.where(hit, -jnp.inf, sc)
            sel_bias = jnp.where(chosen, 0.0, NEG).astype(BF16)
            for x in range(QUERY_GROUP):
                qaug_ref[hl, blk0 + x] = jnp.concatenate(
                    [qts[x], sel_bias[:, x * T:(x + 1) * T], zeros_tail], axis=0)

    def next_item(i, u):
        last = u + 1 == (i + 2) // 2
        return jnp.where(last, i + 1, i), jnp.where(last, 0, u + 1)

    def score(hl, slot, i, u, m):
        i = jnp.minimum(i, nblk - 1)
        row0 = pl.multiple_of(u * 2 * T, 2 * T)
        st = jnp.dot(kaug_ref[hl, pl.ds(row0, 2 * T), :], qaug_ref[hl, i], preferred_element_type=F32)
        s_even = st[0:T] + bias_ref[hl, jnp.clip(i - 2 * u, 0, N_NEAR)]
        s_odd = st[T:] + bias_ref[hl, jnp.clip(i - 2 * u - 1, 0, N_NEAR)]
        s_refs[hl][slot][0:T, :] = s_even
        s_refs[hl][slot][T:, :] = s_odd
        m = jnp.where(u == 0, NEG, m)
        top = jnp.maximum(jnp.max(s_even, axis=0, keepdims=True), jnp.max(s_odd, axis=0, keepdims=True))
        m_new = jnp.maximum(m, top)
        return m_new, jnp.exp2(m - m_new)

    def accumulate(hl, slot, i, u, m, alpha):
        p = jnp.exp2(s_refs[hl][slot][...] - m).astype(BF16)
        acc_ref[hl, i] = acc_ref[hl, i] * alpha + jnp.dot(vaug_ref[hl, u], p, preferred_element_type=F32)

    def item_step(slot, i, u, state):
        i_nxt, u_nxt = next_item(i, u)
        nxt = tuple(score(hl, 1 - slot, i_nxt, u_nxt, state[hl][0]) for hl in heads)
        for hl in heads:
            accumulate(hl, slot, i, u, *state[hl])
        return i_nxt, u_nxt, nxt

    def item_group(_, carry):
        i, u, state = carry
        for n in range(ITEM_UNROLL):
            i, u, state = item_step(n % 2, i, u, state)
        return i, u, state

    acc_ref[...] = jnp.zeros(acc_ref.shape, F32)
    zero = jnp.int32(0)
    m0 = jnp.full((1, T), NEG, F32)
    state = tuple(score(hl, 0, zero, zero, m0) for hl in heads)
    n_items = sum((i + 2) // 2 for i in range(nblk))
    assert n_items % ITEM_UNROLL == 0 and ITEM_UNROLL % 2 == 0
    lax.fori_loop(0, n_items // ITEM_UNROLL, item_group, (zero, zero, state))

    def finish(c, carry):
        for x in range(QUERY_GROUP):
            i = c * QUERY_GROUP + x
            outs = []
            for hl in heads:
                acc = acc_ref[hl, i]
                outs.append(acc[:HEAD_DIM] / acc[HEAD_DIM:HEAD_DIM + 1])
            o_ref[0, pl.ds(pl.multiple_of(i * T, T), T), :] = jnp.concatenate(outs, axis=0).T.astype(BF16)
        return carry

    lax.fori_loop(0, nblk // QUERY_GROUP, finish, 0)


def _moba(qt_b, k_b, vt_b, moba_bias):
    B, nblk = qt_b.shape[0], qt_b.shape[1]
    S = nblk * T
    assert HEAD_DIM + nblk <= AUG, "block one-hot must fit the augmented contraction width"
    assert nblk % KEY_GROUP == 0 and nblk % QUERY_GROUP == 0 and nblk % 2 == 0
    pair = 2 * HEAD_DIM
    return pl.pallas_call(
        functools.partial(_moba_kernel, nblk=nblk),
        grid=(B, MOBA_HEADS // 2),
        in_specs=[
            pl.BlockSpec((1, nblk, pair, T), lambda b, h: (b, 0, h, 0)),
            pl.BlockSpec((1, S, pair), lambda b, h: (b, 0, h)),
            pl.BlockSpec((1, nblk, pair, T), lambda b, h: (b, 0, h, 0)),
            pl.BlockSpec((2, N_NEAR + 1, T, T), lambda b, h: (h, 0, 0, 0)),
        ],
        out_specs=pl.BlockSpec((1, S, pair), lambda b, h: (b, 0, h)),
        out_shape=jax.ShapeDtypeStruct((B, S, MOBA_WIDTH), BF16),
        scratch_shapes=[
            pltpu.VMEM((2, S, AUG), BF16),
            pltpu.VMEM((2, nblk // 2, V_ROWS, 2 * T), BF16),
            pltpu.VMEM((2, nblk, AUG), F32),
            pltpu.VMEM((2, nblk, AUG, T), BF16),
            pltpu.VMEM((2, nblk, V_ROWS, T), F32),
        ] + [pltpu.VMEM((2 * T, T), F32)] * 4,
        compiler_params=pltpu.CompilerParams(
            dimension_semantics=("arbitrary", "arbitrary"), vmem_limit_bytes=VMEM_LIMIT),
        name="moba_attention",
    )(qt_b, k_b, vt_b, moba_bias)


def _sigmoid(v):
    return 1.0 / (1.0 + jnp.exp(-v))


def _rms(v, gain):
    r = lax.rsqrt(jnp.mean(v * v, axis=-1, keepdims=True) + RMS_EPS)
    return (v * r) * gain


def _outproj_kernel(x_ref, ao_ref, bo_ref, ga_ref, gb_ref, p_ref,
                    wo_ref, pg_ref, wg_ref, wp_ref, fg_ref, o_ref, *, final):
    chunk = OUT_ROW_TILE // OUT_CHUNKS
    chunks = [slice(c * chunk, (c + 1) * chunk) for c in range(OUT_CHUNKS)]

    def mix(rows):
        ga = ga_ref[0, rows, :].astype(F32)
        gb = gb_ref[0, rows, :].astype(F32)
        ma = (ao_ref[0, rows, :].astype(F32) * (ga * _sigmoid(ga))).astype(BF16)
        mb = (bo_ref[0, rows, :].astype(F32) * (gb * _sigmoid(gb))).astype(BF16)
        return (x_ref[0, rows, :]
                + jnp.dot(ma, wo_ref[0:SWA_WIDTH, :], preferred_element_type=F32)
                + jnp.dot(mb, wo_ref[SWA_WIDTH:, :], preferred_element_type=F32))

    def gate_up(rows, x1):
        n = _rms(x1, pg_ref[...]).astype(BF16)
        gate = _sigmoid(jnp.dot(n, wg_ref[...], preferred_element_type=F32))
        up = jnp.dot(p_ref[0, rows, :].astype(BF16), wp_ref[...], preferred_element_type=F32)
        return x1 + up * gate

    x1s = [mix(rows) for rows in chunks]
    ys = [gate_up(rows, x1) for rows, x1 in zip(chunks, x1s)]
    for rows, y in zip(chunks, ys):
        o_ref[0, rows, :] = _rms(y, fg_ref[...]) if final else y


def _outproj(x, a_out, b_out, g_a, g_b, p, w_out, ple_gain, w_gate, w_proj, final_gain, final):
    B, S, D = x.shape
    nt = S // OUT_ROW_TILE
    ple = p.shape[-1]

    def tok(width):
        return pl.BlockSpec((1, OUT_ROW_TILE, width), lambda b, t: (b, t, 0))

    def whole(shape):
        return pl.BlockSpec(shape, lambda b, t: (0,) * len(shape))

    return pl.pallas_call(
        functools.partial(_outproj_kernel, final=final),
        grid=(B, nt),
        in_specs=[tok(D), tok(SWA_WIDTH), tok(MOBA_WIDTH), tok(SWA_WIDTH), tok(MOBA_WIDTH), tok(ple),
                  whole((D, D)), whole((1, D)), whole((D, D)), whole((ple, D)), whole((1, D))],
        out_specs=tok(D),
        out_shape=jax.ShapeDtypeStruct((B, S, D), F32),
        compiler_params=pltpu.CompilerParams(
            dimension_semantics=("arbitrary", "arbitrary"), vmem_limit_bytes=VMEM_LIMIT),
        name="outproj_ple",
    )(x, a_out, b_out, g_a, g_b, p, w_out, ple_gain, w_gate, w_proj, final_gain)


def _wprep_kernel(win_ref, wo_ref, wg_ref, wp_ref, wnn_ref, wt_ref, wob_ref, wgb_ref, wpb_ref):
    widths = (SWA_WIDTH, SWA_KV_WIDTH, SWA_KV_WIDTH, SWA_WIDTH,
              MOBA_WIDTH, MOBA_WIDTH, MOBA_WIDTH, MOBA_WIDTH)
    offs = [0]
    for wd in widths:
        offs.append(offs[-1] + wd)
    w = win_ref[...]
    a_q, a_k, a_v, a_g, b_q, b_k, b_v, b_g = (w[:, offs[n]:offs[n + 1]] for n in range(8))
    scale = HEAD_DIM ** -0.5 * LOG2E
    wnn_ref[...] = jnp.concatenate([a_g, b_k, b_g], axis=1).astype(BF16)
    wt_ref[...] = jnp.concatenate([a_k, a_q * scale, a_v, b_q * scale, b_v], axis=1).T.astype(BF16)
    wob_ref[...] = wo_ref[...].astype(BF16)
    wgb_ref[...] = wg_ref[...].astype(BF16)
    wpb_ref[...] = wp_ref[...].astype(BF16)


def _prepare_weights(w_in, w_out, w_gate, w_proj):
    D, n_in = w_in.shape
    ple = w_proj.shape[0]
    steps = D // W_ROWS
    n_nn = SWA_WIDTH + 2 * MOBA_WIDTH
    n_t = n_in - n_nn
    rows = lambda width: pl.BlockSpec((W_ROWS, width), lambda k: (k, 0))
    return pl.pallas_call(
        _wprep_kernel,
        grid=(steps,),
        in_specs=[rows(n_in), rows(D), rows(D), pl.BlockSpec((ple // steps, D), lambda k: (k, 0))],
        out_specs=[rows(n_nn), pl.BlockSpec((n_t, W_ROWS), lambda k: (0, k)), rows(D), rows(D),
                   pl.BlockSpec((ple // steps, D), lambda k: (k, 0))],
        out_shape=[jax.ShapeDtypeStruct((D, n_nn), BF16), jax.ShapeDtypeStruct((n_t, D), BF16),
                   jax.ShapeDtypeStruct((D, D), BF16), jax.ShapeDtypeStruct((D, D), BF16),
                   jax.ShapeDtypeStruct((ple, D), BF16)],
        name="weight_prep",
    )(w_in, w_out, w_gate, w_proj)


def kernel(x, p, norm_in, w_in, sinks, rel_bias, w_out, ple_norm, w_ple_gate, w_ple_proj, final_norm):
    B, S, D = x.shape
    depth = p.shape[0]
    assert S % ROW_TILE == 0 and ROW_TILE % T == 0 and S % OUT_ROW_TILE == 0
    moba_bias, swa_bias = _bias_tables(rel_bias)
    for i in range(depth):
        w_nn, w_t, w_o, w_g, w_p = _prepare_weights(w_in[i], w_out[i], w_ple_gate[i], w_ple_proj[i])
        k_a, g_a, k_b, g_b, qt_a, vt_a, qt_b, vt_b = _inproj(x, norm_in[i][None, :], w_nn, w_t)
        a_out = _swa(sinks[i], qt_a, k_a, vt_a, swa_bias)
        b_out = _moba(qt_b, k_b, vt_b, moba_bias)
        x = _outproj(x, a_out, b_out, g_a, g_b, p[i], w_o, ple_norm[i][None, :], w_g, w_p,
                     final_norm[None, :], final=(i == depth - 1))
    return x
```

```python
import functools
import math

import jax
import jax.numpy as jnp
from jax import lax
from jax.experimental import pallas as pl
from jax.experimental.pallas import tpu as pltpu

F32 = jnp.float32
BF16 = jnp.bfloat16

HEAD_DIM = 64
SWA_Q_HEADS = 8
SWA_KV_HEADS = 2
MOBA_HEADS = 8
WINDOW = 128
MOBA_BLOCK = 256
MOBA_TOPK = 3
NUM_BUCKETS = 32
MAX_DISTANCE = 1024
RMS_EPS = 1e-6

SWA_WIDTH = SWA_Q_HEADS * HEAD_DIM
SWA_KV_WIDTH = SWA_KV_HEADS * HEAD_DIM
MOBA_WIDTH = MOBA_HEADS * HEAD_DIM
SWA_GROUP = SWA_Q_HEADS // SWA_KV_HEADS

T = MOBA_BLOCK
AUG = 128
V_ROWS = HEAD_DIM + 16
NEG = -1e30
LOG2E = math.log2(math.e)
N_NEAR = -(-(MAX_DISTANCE + T) // T)
KEY_GROUP = 8
QUERY_GROUP = 4
ITEM_UNROLL = 34
assert WINDOW <= T // 2
SWA_SPLIT = 2
SWA_UNROLL = 8
ROW_TILE = 1024
W_ROWS = 128
OUT_ROW_TILE = 1024
OUT_CHUNKS = 8
VMEM_LIMIT = 56 * 1024 * 1024


def _bucket_starts():
    max_exact = NUM_BUCKETS // 2
    starts = list(range(max_exact))
    for j in range(NUM_BUCKETS - max_exact):
        edge = max_exact * (MAX_DISTANCE / max_exact) ** (j / (NUM_BUCKETS - max_exact))
        starts.append(math.ceil(edge - 1e-9))
    return starts


BUCKET_STARTS = _bucket_starts()


def _bias_of_distance(dist, table, dist_lo, dist_hi):
    bucket = lambda n: max(b for b in range(NUM_BUCKETS) if BUCKET_STARTS[b] <= max(n, 0))
    lo, hi = bucket(dist_lo), bucket(dist_hi)
    out = jnp.full(dist.shape, table(lo), F32)
    for b in range(lo + 1, hi + 1):
        out = jnp.where(dist >= BUCKET_STARTS[b], table(b), out)
    return out


def _bias_kernel(rel_ref, moba_ref, swa_ref):
    h = pl.program_id(0)
    kk = lax.broadcasted_iota(jnp.int32, (T, T), 0)
    qq = lax.broadcasted_iota(jnp.int32, (T, T), 1)
    rel = qq - kk

    def lookup(dist, col, dist_lo, dist_hi):
        return _bias_of_distance(dist, lambda b: rel_ref[b, col] * LOG2E, dist_lo, dist_hi)

    for d in range(N_NEAR):
        dist = rel + d * T
        bias = lookup(dist, SWA_Q_HEADS + h, max(d * T - (T - 1), 0), d * T + (T - 1))
        if d == 0:
            bias = jnp.where(dist >= 0, bias, NEG)
        moba_ref[0, d] = bias
    far = jnp.full((T, T), N_NEAR * T, jnp.int32)
    moba_ref[0, N_NEAR] = lookup(far, SWA_Q_HEADS + h, N_NEAR * T, N_NEAR * T)

    for d in range(2):
        dist = rel + d * T
        bias = lookup(dist, h, 0, WINDOW - 1)
        valid = (dist >= 0) & (dist < WINDOW)
        swa_ref[0, d] = jnp.where(valid, bias, NEG)


def _bias_tables(rel_bias):
    assert MOBA_HEADS == SWA_Q_HEADS
    return pl.pallas_call(
        _bias_kernel,
        grid=(MOBA_HEADS,),
        in_specs=[pl.BlockSpec(memory_space=pltpu.SMEM)],
        out_specs=[
            pl.BlockSpec((1, N_NEAR + 1, T, T), lambda h: (h, 0, 0, 0)),
            pl.BlockSpec((1, 2, T, T), lambda h: (h, 0, 0, 0)),
        ],
        out_shape=[
            jax.ShapeDtypeStruct((MOBA_HEADS, N_NEAR + 1, T, T), F32),
            jax.ShapeDtypeStruct((SWA_Q_HEADS, 2, T, T), F32),
        ],
        name="bias_tables",
    )(rel_bias)


def _inproj_kernel(x_ref, g_ref, wnn_ref, wt_ref,
                   ka_ref, ga_ref, kb_ref, gb_ref,
                   qta_ref, vta_ref, qtb_ref, vtb_ref):
    x = x_ref[0]
    r = lax.rsqrt(jnp.mean(x * x, axis=-1, keepdims=True) + RMS_EPS)
    h = ((x * r) * g_ref[...]).astype(BF16)

    tok = jnp.dot(h, wnn_ref[...], preferred_element_type=F32)
    o = 0
    for ref, width in ((ga_ref, SWA_WIDTH), (kb_ref, MOBA_WIDTH), (gb_ref, MOBA_WIDTH)):
        ref[0] = tok[:, o:o + width].astype(BF16)
        o += width

    feat = lax.dot_general(wt_ref[...], h, (((1,), (1,)), ((), ())),
                           preferred_element_type=F32)
    ka_ref[0] = feat[0:SWA_KV_WIDTH, :].T.astype(BF16)
    o = SWA_KV_WIDTH
    for ref, width in ((qta_ref, SWA_WIDTH), (vta_ref, SWA_KV_WIDTH),
                       (qtb_ref, MOBA_WIDTH), (vtb_ref, MOBA_WIDTH)):
        for c in range(ROW_TILE // T):
            ref[0, c] = feat[o:o + width, c * T:(c + 1) * T].astype(BF16)
        o += width


def _inproj(x, gain, w_nn, w_t):
    B, S, D = x.shape
    nt = S // ROW_TILE
    cpt = ROW_TILE // T
    n_nn = w_nn.shape[1]
    n_t = w_t.shape[0]

    def tok_spec(width):
        return pl.BlockSpec((1, ROW_TILE, width), lambda b, t: (b, t, 0))

    def feat_spec(width):
        return pl.BlockSpec((1, cpt, width, T), lambda b, t: (b, t, 0, 0))

    def tok_shape(width):
        return jax.ShapeDtypeStruct((B, S, width), BF16)

    def feat_shape(width):
        return jax.ShapeDtypeStruct((B, S // T, width, T), BF16)

    return pl.pallas_call(
        _inproj_kernel,
        grid=(B, nt),
        in_specs=[
            pl.BlockSpec((1, ROW_TILE, D), lambda b, t: (b, t, 0)),
            pl.BlockSpec((1, D), lambda b, t: (0, 0)),
            pl.BlockSpec((D, n_nn), lambda b, t: (0, 0)),
            pl.BlockSpec((n_t, D), lambda b, t: (0, 0)),
        ],
        out_specs=[tok_spec(SWA_KV_WIDTH), tok_spec(SWA_WIDTH), tok_spec(MOBA_WIDTH), tok_spec(MOBA_WIDTH),
                   feat_spec(SWA_WIDTH), feat_spec(SWA_KV_WIDTH), feat_spec(MOBA_WIDTH), feat_spec(MOBA_WIDTH)],
        out_shape=[tok_shape(SWA_KV_WIDTH), tok_shape(SWA_WIDTH), tok_shape(MOBA_WIDTH), tok_shape(MOBA_WIDTH),
                   feat_shape(SWA_WIDTH), feat_shape(SWA_KV_WIDTH), feat_shape(MOBA_WIDTH), feat_shape(MOBA_WIDTH)],
        compiler_params=pltpu.CompilerParams(
            dimension_semantics=("arbitrary", "arbitrary"), vmem_limit_bytes=VMEM_LIMIT),
        name="norm_inproj",
    )(x, gain, w_nn, w_t)


def _ones_pad_rows(width):
    row = lax.broadcasted_iota(jnp.int32, (V_ROWS - HEAD_DIM, width), 0)
    return jnp.where(row == 0, 1.0, 0.0).astype(BF16)


def _swa_kernel(sink_ref, qt_ref, k_ref, vt_ref, bias_ref, o_ref,
                cur0_ref, prev0_ref, cur1_ref, prev1_ref, *, per_step):
    bufs = ((cur0_ref, prev0_ref), (cur1_ref, prev1_ref))
    half = T // 2
    base = pl.program_id(1) * per_step
    zeros_q = jnp.zeros((HEAD_DIM, T), BF16)
    head_rows = [slice(h * HEAD_DIM, (h + 1) * HEAD_DIM) for h in range(SWA_Q_HEADS)]

    def score(slot, t):
        t = jnp.minimum(t, per_step - 1)
        blk = base + t
        row0 = pl.multiple_of(blk * T, T)
        k_cur = k_ref[0, pl.ds(row0, T), :]
        k_prev = k_ref[0, pl.ds(pl.multiple_of(jnp.maximum(row0 - half, 0), half), half), :]
        prev_pen = jnp.where(blk == 0, NEG, 0.0).astype(F32)
        maxes = []
        for head in range(SWA_Q_HEADS):
            parts = [zeros_q] * SWA_KV_HEADS
            parts[head // SWA_GROUP] = qt_ref[0, t, head_rows[head], :]
            q_pad = jnp.concatenate(parts, axis=0)
            s_cur = jnp.dot(k_cur, q_pad, preferred_element_type=F32) + bias_ref[head, 0]
            s_prev = (jnp.dot(k_prev, q_pad, preferred_element_type=F32)
                      + (bias_ref[head, 1, half:, :] + prev_pen))
            bufs[slot][0][head] = s_cur
            bufs[slot][1][head] = s_prev
            m = jnp.maximum(jnp.max(s_cur, axis=0, keepdims=True), jnp.max(s_prev, axis=0, keepdims=True))
            maxes.append(jnp.maximum(m, sink_ref[head] * LOG2E))
        return tuple(maxes)

    def finish(slot, t, maxes):
        blk = base + t
        blk_prev = jnp.maximum(blk - 1, 0)
        outs = []
        for g in range(SWA_KV_HEADS):
            v_cur = jnp.concatenate([vt_ref[0, blk, head_rows[g], :], _ones_pad_rows(T)], axis=0)
            v_prev = jnp.concatenate([vt_ref[0, blk_prev, head_rows[g], half:], _ones_pad_rows(half)], axis=0)
            for head in range(g * SWA_GROUP, (g + 1) * SWA_GROUP):
                m = maxes[head]
                p_cur = jnp.exp2(bufs[slot][0][head] - m).astype(BF16)
                p_prev = jnp.exp2(bufs[slot][1][head] - m).astype(BF16)
                acc = (jnp.dot(v_cur, p_cur, preferred_element_type=F32)
                       + jnp.dot(v_prev, p_prev, preferred_element_type=F32))
                den = acc[HEAD_DIM:HEAD_DIM + 1] + jnp.exp2(sink_ref[head] * LOG2E - m)
                outs.append(acc[:HEAD_DIM] / den)
        o_ref[0, pl.ds(pl.multiple_of(t * T, T), T), :] = jnp.concatenate(outs, axis=0).T.astype(BF16)

    def block_group(c, maxes):
        for n in range(SWA_UNROLL):
            t = c * SWA_UNROLL + n
            nxt = score((n + 1) % 2, t + 1)
            finish(n % 2, t, maxes)
            maxes = nxt
        return maxes

    lax.fori_loop(0, per_step // SWA_UNROLL, block_group, score(0, jnp.int32(0)))


def _swa(sinks, qt_a, k_a, vt_a, swa_bias):
    B, nblk = qt_a.shape[0], qt_a.shape[1]
    S = nblk * T
    per_step = nblk // SWA_SPLIT
    assert nblk % SWA_SPLIT == 0 and per_step % SWA_UNROLL == 0 and SWA_UNROLL % 2 == 0
    return pl.pallas_call(
        functools.partial(_swa_kernel, per_step=per_step),
        grid=(B, SWA_SPLIT),
        in_specs=[
            pl.BlockSpec(memory_space=pltpu.SMEM),
            pl.BlockSpec((1, per_step, SWA_WIDTH, T), lambda b, h: (b, h, 0, 0)),
            pl.BlockSpec((1, S, SWA_KV_WIDTH), lambda b, h: (b, 0, 0)),
            pl.BlockSpec((1, nblk, SWA_KV_WIDTH, T), lambda b, h: (b, 0, 0, 0)),
            pl.BlockSpec((SWA_Q_HEADS, 2, T, T), lambda b, h: (0, 0, 0, 0)),
        ],
        out_specs=pl.BlockSpec((1, per_step * T, SWA_WIDTH), lambda b, h: (b, h, 0)),
        out_shape=jax.ShapeDtypeStruct((B, S, SWA_WIDTH), BF16),
        scratch_shapes=[pltpu.VMEM((SWA_Q_HEADS, T, T), F32),
                        pltpu.VMEM((SWA_Q_HEADS, T // 2, T), F32)] * 2,
        compiler_params=pltpu.CompilerParams(
            dimension_semantics=("arbitrary", "arbitrary"), vmem_limit_bytes=VMEM_LIMIT),
        name="swa_attention",
    )(sinks, qt_a, k_a, vt_a, swa_bias)


def _moba_kernel(qt_ref, k_ref, vt_ref, bias_ref, o_ref,
                 kaug_ref, vaug_ref, kmean_ref, qaug_ref, acc_ref,
                 s00_ref, s01_ref, s10_ref, s11_ref, *, nblk):
    s_refs = ((s00_ref, s01_ref), (s10_ref, s11_ref))
    heads = range(2)
    head_rows = [slice(hl * HEAD_DIM, (hl + 1) * HEAD_DIM) for hl in heads]

    rows = KEY_GROUP * T
    sel_r = lax.broadcasted_iota(jnp.int32, (2 * HEAD_DIM, AUG), 0)
    sel_c = lax.broadcasted_iota(jnp.int32, (2 * HEAD_DIM, AUG), 1)
    picks = [jnp.where((sel_r == sel_c + hl * HEAD_DIM) & (sel_c < HEAD_DIM), 1.0, 0.0).astype(BF16)
             for hl in heads]
    lane = lax.broadcasted_iota(jnp.int32, (rows, AUG), 1)
    row_blk = lax.shift_right_logical(lax.broadcasted_iota(jnp.int32, (rows, AUG), 0), T.bit_length() - 1)
    pad = _ones_pad_rows(T)

    def prep(c, carry):
        row0 = pl.multiple_of(c * rows, rows)
        blk0 = pl.multiple_of(c * KEY_GROUP, KEY_GROUP)
        kb = k_ref[0, pl.ds(row0, rows), :]
        onehot = jnp.where(lane == HEAD_DIM + blk0 + row_blk, 1.0, 0.0)
        for hl in heads:
            kp = jnp.dot(kb, picks[hl], preferred_element_type=F32)
            kaug_ref[hl, pl.ds(row0, rows), :] = (kp + onehot).astype(BF16)
            kmean_ref[hl, pl.ds(blk0, KEY_GROUP), :] = (
                jnp.sum(kp.reshape(KEY_GROUP, T, AUG), axis=1) * (1.0 / T))
            for r in range(KEY_GROUP):
                vaug_ref[hl, c * (KEY_GROUP // 2) + r // 2, :, (r % 2) * T:(r % 2 + 1) * T] = (
                    jnp.concatenate([vt_ref[0, blk0 + r, head_rows[hl], :], pad], axis=0))
        return carry

    lax.fori_loop(0, nblk // KEY_GROUP, prep, 0)

    width = QUERY_GROUP * T
    zeros_q = jnp.zeros((AUG - HEAD_DIM, T), BF16)
    kmeans = []
    for hl in heads:
        km = kmean_ref[hl]
        km_hi = km.astype(BF16)
        kmeans.append((km_hi, (km - km_hi.astype(F32)).astype(BF16)))

    for blk0 in range(0, nblk, QUERY_GROUP):
        rows_used = min(-(-(blk0 + QUERY_GROUP) // 16) * 16, nblk)
        blk_id = lax.broadcasted_iota(jnp.int32, (rows_used, width), 0)
        q_blk = blk0 + lax.shift_right_logical(
            lax.broadcasted_iota(jnp.int32, (rows_used, width), 1), T.bit_length() - 1)
        zeros_tail = jnp.zeros((AUG - HEAD_DIM - rows_used, T), BF16)
        for hl in heads:
            km_hi, km_lo = (km[:rows_used] for km in kmeans[hl])
            qts = [qt_ref[0, blk0 + x, head_rows[hl], :] for x in range(QUERY_GROUP)]
            scs = []
            for qt in qts:
                qs = jnp.concatenate([qt, zeros_q], axis=0)
                scs.append(jnp.dot(km_hi, qs, preferred_element_type=F32)
                           + jnp.dot(km_lo, qs, preferred_element_type=F32))
            sc = jnp.concatenate(scs, axis=1)
            sc = jnp.where(blk_id < q_blk, sc, -jnp.inf)
            chosen = blk_id == q_blk
            for _ in range(MOBA_TOPK):
                mx = jnp.max(sc, axis=0, keepdims=True)
                first = jnp.min(jnp.where(sc == mx, blk_id, nblk), axis=0, keepdims=True)
                hit = (blk_id == first) & (mx > -jnp.inf)
                chosen = chosen | hit
                sc = jnp.where(hit, -jnp.inf, sc)
            sel_bias = jnp.where(chosen, 0.0, NEG).astype(BF16)
            for x in range(QUERY_GROUP):
                qaug_ref[hl, blk0 + x] = jnp.concatenate(
                    [qts[x], sel_bias[:, x * T:(x + 1) * T], zeros_tail], axis=0)

    def next_item(i, u):
        last = u + 1 == (i + 2) // 2
        return jnp.where(last, i + 1, i), jnp.where(last, 0, u + 1)

    def score(hl, slot, i, u, m):
        i = jnp.minimum(i, nblk - 1)
        row0 = pl.multiple_of(u * 2 * T, 2 * T)
        st = jnp.dot(kaug_ref[hl, pl.ds(row0, 2 * T), :], qaug_ref[hl, i], preferred_element_type=F32)
        s_even = st[0:T] + bias_ref[hl, jnp.clip(i - 2 * u, 0, N_NEAR)]
        s_odd = st[T:] + bias_ref[hl, jnp.clip(i - 2 * u - 1, 0, N_NEAR)]
        s_refs[hl][slot][0:T, :] = s_even
        s_refs[hl][slot][T:, :] = s_odd
        m = jnp.where(u == 0, NEG, m)
        top = jnp.maximum(jnp.max(s_even, axis=0, keepdims=True), jnp.max(s_odd, axis=0, keepdims=True))
        m_new = jnp.maximum(m, top)
        return m_new, jnp.exp2(m - m_new)

    def accumulate(hl, slot, i, u, m, alpha):
        p = jnp.exp2(s_refs[hl][slot][...] - m).astype(BF16)
        acc_ref[hl, i] = acc_ref[hl, i] * alpha + jnp.dot(vaug_ref[hl, u], p, preferred_element_type=F32)

    def item_step(slot, i, u, state):
        i_nxt, u_nxt = next_item(i, u)
        nxt = tuple(score(hl, 1 - slot, i_nxt, u_nxt, state[hl][0]) for hl in heads)
        for hl in heads:
            accumulate(hl, slot, i, u, *state[hl])
        return i_nxt, u_nxt, nxt

    def item_group(_, carry):
        i, u, state = carry
        for n in range(ITEM_UNROLL):
            i, u, state = item_step(n % 2, i, u, state)
        return i, u, state

    acc_ref[...] = jnp.zeros(acc_ref.shape, F32)
    zero = jnp.int32(0)
    m0 = jnp.full((1, T), NEG, F32)
    state = tuple(score(hl, 0, zero, zero, m0) for hl in heads)
    n_items = sum((i + 2) // 2 for i in range(nblk))
    assert n_items % ITEM_UNROLL == 0 and ITEM_UNROLL % 2 == 0
    lax.fori_loop(0, n_items // ITEM_UNROLL, item_group, (zero, zero, state))

    def finish(c, carry):
        for x in range(QUERY_GROUP):
            i = c * QUERY_GROUP + x
            outs = []
            for hl in heads:
                acc = acc_ref[hl, i]
                outs.append(acc[:HEAD_DIM] / acc[HEAD_DIM:HEAD_DIM + 1])
            o_ref[0, pl.ds(pl.multiple_of(i * T, T), T), :] = jnp.concatenate(outs, axis=0).T.astype(BF16)
        return carry

    lax.fori_loop(0, nblk // QUERY_GROUP, finish, 0)


def _moba(qt_b, k_b, vt_b, moba_bias):
    B, nblk = qt_b.shape[0], qt_b.shape[1]
    S = nblk * T
    assert HEAD_DIM + nblk <= AUG, "block one-hot must fit the augmented contraction width"
    assert nblk % KEY_GROUP == 0 and nblk % QUERY_GROUP == 0 and nblk % 2 == 0
    pair = 2 * HEAD_DIM
    return pl.pallas_call(
        functools.partial(_moba_kernel, nblk=nblk),
        grid=(B, MOBA_HEADS // 2),
        in_specs=[
            pl.BlockSpec((1, nblk, pair, T), lambda b, h: (b, 0, h, 0)),
            pl.BlockSpec((1, S, pair), lambda b, h: (b, 0, h)),
            pl.BlockSpec((1, nblk, pair, T), lambda b, h: (b, 0, h, 0)),
            pl.BlockSpec((2, N_NEAR + 1, T, T), lambda b, h: (h, 0, 0, 0)),
        ],
        out_specs=pl.BlockSpec((1, S, pair), lambda b, h: (b, 0, h)),
        out_shape=jax.ShapeDtypeStruct((B, S, MOBA_WIDTH), BF16),
        scratch_shapes=[
            pltpu.VMEM((2, S, AUG), BF16),
            pltpu.VMEM((2, nblk // 2, V_ROWS, 2 * T), BF16),
            pltpu.VMEM((2, nblk, AUG), F32),
            pltpu.VMEM((2, nblk, AUG, T), BF16),
            pltpu.VMEM((2, nblk, V_ROWS, T), F32),
        ] + [pltpu.VMEM((2 * T, T), F32)] * 4,
        compiler_params=pltpu.CompilerParams(
            dimension_semantics=("arbitrary", "arbitrary"), vmem_limit_bytes=VMEM_LIMIT),
        name="moba_attention",
    )(qt_b, k_b, vt_b, moba_bias)


def _sigmoid(v):
    return 1.0 / (1.0 + jnp.exp(-v))


def _rms(v, gain):
    r = lax.rsqrt(jnp.mean(v * v, axis=-1, keepdims=True) + RMS_EPS)
    return (v * r) * gain


def _outproj_kernel(x_ref, ao_ref, bo_ref, ga_ref, gb_ref, p_ref,
                    wo_ref, pg_ref, wg_ref, wp_ref, fg_ref, o_ref, *, final):
    chunk = OUT_ROW_TILE // OUT_CHUNKS
    chunks = [slice(c * chunk, (c + 1) * chunk) for c in range(OUT_CHUNKS)]

    def mix(rows):
        ga = ga_ref[0, rows, :].astype(F32)
        gb = gb_ref[0, rows, :].astype(F32)
        ma = (ao_ref[0, rows, :].astype(F32) * (ga * _sigmoid(ga))).astype(BF16)
        mb = (bo_ref[0, rows, :].astype(F32) * (gb * _sigmoid(gb))).astype(BF16)
        return (x_ref[0, rows, :]
                + jnp.dot(ma, wo_ref[0:SWA_WIDTH, :], preferred_element_type=F32)
                + jnp.dot(mb, wo_ref[SWA_WIDTH:, :], preferred_element_type=F32))

    def gate_up(rows, x1):
        n = _rms(x1, pg_ref[...]).astype(BF16)
        gate = _sigmoid(jnp.dot(n, wg_ref[...], preferred_element_type=F32))
        up = jnp.dot(p_ref[0, rows, :].astype(BF16), wp_ref[...], preferred_element_type=F32)
        return x1 + up * gate

    x1s = [mix(rows) for rows in chunks]
    ys = [gate_up(rows, x1) for rows, x1 in zip(chunks, x1s)]
    for rows, y in zip(chunks, ys):
        o_ref[0, rows, :] = _rms(y, fg_ref[...]) if final else y


def _outproj(x, a_out, b_out, g_a, g_b, p, w_out, ple_gain, w_gate, w_proj, final_gain, final):
    B, S, D = x.shape
    nt = S // OUT_ROW_TILE
    ple = p.shape[-1]

    def tok(width):
        return pl.BlockSpec((1, OUT_ROW_TILE, width), lambda b, t: (b, t, 0))

    def whole(shape):
        return pl.BlockSpec(shape, lambda b, t: (0,) * len(shape))

    return pl.pallas_call(
        functools.partial(_outproj_kernel, final=final),
        grid=(B, nt),
        in_specs=[tok(D), tok(SWA_WIDTH), tok(MOBA_WIDTH), tok(SWA_WIDTH), tok(MOBA_WIDTH), tok(ple),
                  whole((D, D)), whole((1, D)), whole((D, D)), whole((ple, D)), whole((1, D))],
        out_specs=tok(D),
        out_shape=jax.ShapeDtypeStruct((B, S, D), F32),
        compiler_params=pltpu.CompilerParams(
            dimension_semantics=("arbitrary", "arbitrary"), vmem_limit_bytes=VMEM_LIMIT),
        name="outproj_ple",
    )(x, a_out, b_out, g_a, g_b, p, w_out, ple_gain, w_gate, w_proj, final_gain)


def _wprep_kernel(win_ref, wo_ref, wg_ref, wp_ref, wnn_ref, wt_ref, wob_ref, wgb_ref, wpb_ref):
    widths = (SWA_WIDTH, SWA_KV_WIDTH, SWA_KV_WIDTH, SWA_WIDTH,
              MOBA_WIDTH, MOBA_WIDTH, MOBA_WIDTH, MOBA_WIDTH)
    offs = [0]
    for wd in widths:
        offs.append(offs[-1] + wd)
    w = win_ref[...]
    a_q, a_k, a_v, a_g, b_q, b_k, b_v, b_g = (w[:, offs[n]:offs[n + 1]] for n in range(8))
    scale = HEAD_DIM ** -0.5 * LOG2E
    wnn_ref[...] = jnp.concatenate([a_g, b_k, b_g], axis=1).astype(BF16)
    wt_ref[...] = jnp.concatenate([a_k, a_q * scale, a_v, b_q * scale, b_v], axis=1).T.astype(BF16)
    wob_ref[...] = wo_ref[...].astype(BF16)
    wgb_ref[...] = wg_ref[...].astype(BF16)
    wpb_ref[...] = wp_ref[...].astype(BF16)


def _prepare_weights(w_in, w_out, w_gate, w_proj):
    D, n_in = w_in.shape
    ple = w_proj.shape[0]
    steps = D // W_ROWS
    n_nn = SWA_WIDTH + 2 * MOBA_WIDTH
    n_t = n_in - n_nn
    rows = lambda width: pl.BlockSpec((W_ROWS, width), lambda k: (k, 0))
    return pl.pallas_call(
        _wprep_kernel,
        grid=(steps,),
        in_specs=[rows(n_in), rows(D), rows(D), pl.BlockSpec((ple // steps, D), lambda k: (k, 0))],
        out_specs=[rows(n_nn), pl.BlockSpec((n_t, W_ROWS), lambda k: (0, k)), rows(D), rows(D),
                   pl.BlockSpec((ple // steps, D), lambda k: (k, 0))],
        out_shape=[jax.ShapeDtypeStruct((D, n_nn), BF16), jax.ShapeDtypeStruct((n_t, D), BF16),
                   jax.ShapeDtypeStruct((D, D), BF16), jax.ShapeDtypeStruct((D, D), BF16),
                   jax.ShapeDtypeStruct((ple, D), BF16)],
        name="weight_prep",
    )(w_in, w_out, w_gate, w_proj)


def kernel(x, p, norm_in, w_in, sinks, rel_bias, w_out, ple_norm, w_ple_gate, w_ple_proj, final_norm):
    B, S, D = x.shape
    depth = p.shape[0]
    assert S % ROW_TILE == 0 and ROW_TILE % T == 0 and S % OUT_ROW_TILE == 0
    moba_bias, swa_bias = _bias_tables(rel_bias)
    for i in range(depth):
        w_nn, w_t, w_o, w_g, w_p = _prepare_weights(w_in[i], w_out[i], w_ple_gate[i], w_ple_proj[i])
        k_a, g_a, k_b, g_b, qt_a, vt_a, qt_b, vt_b = _inproj(x, norm_in[i][None, :], w_nn, w_t)
        a_out = _swa(sinks[i], qt_a, k_a, vt_a, swa_bias)
        b_out = _moba(qt_b, k_b, vt_b, moba_bias)
        x = _outproj(x, a_out, b_out, g_a, g_b, p[i], w_o, ple_norm[i][None, :], w_g, w_p,
                     final_norm[None, :], final=(i == depth - 1))
    return x
```

```python
import functools
import math

import jax
import jax.numpy as jnp
from jax import lax
from jax.experimental import pallas as pl
from jax.experimental.pallas import tpu as pltpu

F32 = jnp.float32
BF16 = jnp.bfloat16

HEAD_DIM = 64
SWA_Q_HEADS = 8
SWA_KV_HEADS = 2
MOBA_HEADS = 8
WINDOW = 128
MOBA_BLOCK = 256
MOBA_TOPK = 3
NUM_BUCKETS = 32
MAX_DISTANCE = 1024
RMS_EPS = 1e-6

SWA_WIDTH = SWA_Q_HEADS * HEAD_DIM
SWA_KV_WIDTH = SWA_KV_HEADS * HEAD_DIM
MOBA_WIDTH = MOBA_HEADS * HEAD_DIM
SWA_GROUP = SWA_Q_HEADS // SWA_KV_HEADS

T = MOBA_BLOCK
AUG = 128
V_ROWS = HEAD_DIM + 16
NEG = -1e30
LOG2E = math.log2(math.e)
N_NEAR = -(-(MAX_DISTANCE + T) // T)
KEY_GROUP = 8
QUERY_GROUP = 4
ITEM_UNROLL = 34
assert WINDOW <= T // 2
SWA_SPLIT = 2
SWA_UNROLL = 8
ROW_TILE = 1024
W_ROWS = 128
OUT_ROW_TILE = 1024
OUT_CHUNKS = 8
VMEM_LIMIT = 56 * 1024 * 1024


def _bucket_starts():
    max_exact = NUM_BUCKETS // 2
    starts = list(range(max_exact))
    for j in range(NUM_BUCKETS - max_exact):
        edge = max_exact * (MAX_DISTANCE / max_exact) ** (j / (NUM_BUCKETS - max_exact))
        starts.append(math.ceil(edge - 1e-9))
    return starts


BUCKET_STARTS = _bucket_starts()


def _bias_of_distance(dist, table, dist_lo, dist_hi):
    bucket = lambda n: max(b for b in range(NUM_BUCKETS) if BUCKET_STARTS[b] <= max(n, 0))
    lo, hi = bucket(dist_lo), bucket(dist_hi)
    out = jnp.full(dist.shape, table(lo), F32)
    for b in range(lo + 1, hi + 1):
        out = jnp.where(dist >= BUCKET_STARTS[b], table(b), out)
    return out


def _bias_kernel(rel_ref, moba_ref, swa_ref):
    h = pl.program_id(0)
    kk = lax.broadcasted_iota(jnp.int32, (T, T), 0)
    qq = lax.broadcasted_iota(jnp.int32, (T, T), 1)
    rel = qq - kk

    def lookup(dist, col, dist_lo, dist_hi):
        return _bias_of_distance(dist, lambda b: rel_ref[b, col] * LOG2E, dist_lo, dist_hi)

    for d in range(N_NEAR):
        dist = rel + d * T
        bias = lookup(dist, SWA_Q_HEADS + h, max(d * T - (T - 1), 0), d * T + (T - 1))
        if d == 0:
            bias = jnp.where(dist >= 0, bias, NEG)
        moba_ref[0, d] = bias
    far = jnp.full((T, T), N_NEAR * T, jnp.int32)
    moba_ref[0, N_NEAR] = lookup(far, SWA_Q_HEADS + h, N_NEAR * T, N_NEAR * T)

    for d in range(2):
        dist = rel + d * T
        bias = lookup(dist, h, 0, WINDOW - 1)
        valid = (dist >= 0) & (dist < WINDOW)
        swa_ref[0, d] = jnp.where(valid, bias, NEG)


def _bias_tables(rel_bias):
    assert MOBA_HEADS == SWA_Q_HEADS
    return pl.pallas_call(
        _bias_kernel,
        grid=(MOBA_HEADS,),
        in_specs=[pl.BlockSpec(memory_space=pltpu.SMEM)],
        out_specs=[
            pl.BlockSpec((1, N_NEAR + 1, T, T), lambda h: (h, 0, 0, 0)),
            pl.BlockSpec((1, 2, T, T), lambda h: (h, 0, 0, 0)),
        ],
        out_shape=[
            jax.ShapeDtypeStruct((MOBA_HEADS, N_NEAR + 1, T, T), F32),
            jax.ShapeDtypeStruct((SWA_Q_HEADS, 2, T, T), F32),
        ],
        name="bias_tables",
    )(rel_bias)


def _inproj_kernel(x_ref, g_ref, wnn_ref, wt_ref,
                   ka_ref, ga_ref, kb_ref, gb_ref,
                   qta_ref, vta_ref, qtb_ref, vtb_ref):
    x = x_ref[0]
    r = lax.rsqrt(jnp.mean(x * x, axis=-1, keepdims=True) + RMS_EPS)
    h = ((x * r) * g_ref[...]).astype(BF16)

    tok = jnp.dot(h, wnn_ref[...], preferred_element_type=F32)
    o = 0
    for ref, width in ((ga_ref, SWA_WIDTH), (kb_ref, MOBA_WIDTH), (gb_ref, MOBA_WIDTH)):
        ref[0] = tok[:, o:o + width].astype(BF16)
        o += width

    feat = lax.dot_general(wt_ref[...], h, (((1,), (1,)), ((), ())),
                           preferred_element_type=F32)
    ka_ref[0] = feat[0:SWA_KV_WIDTH, :].T.astype(BF16)
    o = SWA_KV_WIDTH
    for ref, width in ((qta_ref, SWA_WIDTH), (vta_ref, SWA_KV_WIDTH),
                       (qtb_ref, MOBA_WIDTH), (vtb_ref, MOBA_WIDTH)):
        for c in range(ROW_TILE // T):
            ref[0, c] = feat[o:o + width, c * T:(c + 1) * T].astype(BF16)
        o += width


def _inproj(x, gain, w_nn, w_t):
    B, S, D = x.shape
    nt = S // ROW_TILE
    cpt = ROW_TILE // T
    n_nn = w_nn.shape[1]
    n_t = w_t.shape[0]

    def tok_spec(width):
        return pl.BlockSpec((1, ROW_TILE, width), lambda b, t: (b, t, 0))

    def feat_spec(width):
        return pl.BlockSpec((1, cpt, width, T), lambda b, t: (b, t, 0, 0))

    def tok_shape(width):
        return jax.ShapeDtypeStruct((B, S, width), BF16)

    def feat_shape(width):
        return jax.ShapeDtypeStruct((B, S // T, width, T), BF16)

    return pl.pallas_call(
        _inproj_kernel,
        grid=(B, nt),
        in_specs=[
            pl.BlockSpec((1, ROW_TILE, D), lambda b, t: (b, t, 0)),
            pl.BlockSpec((1, D), lambda b, t: (0, 0)),
            pl.BlockSpec((D, n_nn), lambda b, t: (0, 0)),
            pl.BlockSpec((n_t, D), lambda b, t: (0, 0)),
        ],
        out_specs=[tok_spec(SWA_KV_WIDTH), tok_spec(SWA_WIDTH), tok_spec(MOBA_WIDTH), tok_spec(MOBA_WIDTH),
                   feat_spec(SWA_WIDTH), feat_spec(SWA_KV_WIDTH), feat_spec(MOBA_WIDTH), feat_spec(MOBA_WIDTH)],
        out_shape=[tok_shape(SWA_KV_WIDTH), tok_shape(SWA_WIDTH), tok_shape(MOBA_WIDTH), tok_shape(MOBA_WIDTH),
                   feat_shape(SWA_WIDTH), feat_shape(SWA_KV_WIDTH), feat_shape(MOBA_WIDTH), feat_shape(MOBA_WIDTH)],
        compiler_params=pltpu.CompilerParams(
            dimension_semantics=("arbitrary", "arbitrary"), vmem_limit_bytes=VMEM_LIMIT),
        name="norm_inproj",
    )(x, gain, w_nn, w_t)


def _ones_pad_rows(width):
    row = lax.broadcasted_iota(jnp.int32, (V_ROWS - HEAD_DIM, width), 0)
    return jnp.where(row == 0, 1.0, 0.0).astype(BF16)


def _swa_kernel(sink_ref, qt_ref, k_ref, vt_ref, bias_ref, o_ref,
                cur0_ref, prev0_ref, cur1_ref, prev1_ref, *, per_step):
    bufs = ((cur0_ref, prev0_ref), (cur1_ref, prev1_ref))
    half = T // 2
    base = pl.program_id(1) * per_step
    zeros_q = jnp.zeros((HEAD_DIM, T), BF16)
    head_rows = [slice(h * HEAD_DIM, (h + 1) * HEAD_DIM) for h in range(SWA_Q_HEADS)]

    def score(slot, t):
        t = jnp.minimum(t, per_step - 1)
        blk = base + t
        row0 = pl.multiple_of(blk * T, T)
        k_cur = k_ref[0, pl.ds(row0, T), :]
        k_prev = k_ref[0, pl.ds(pl.multiple_of(jnp.maximum(row0 - half, 0), half), half), :]
        prev_pen = jnp.where(blk == 0, NEG, 0.0).astype(F32)
        maxes = []
        for head in range(SWA_Q_HEADS):
            parts = [zeros_q] * SWA_KV_HEADS
            parts[head // SWA_GROUP] = qt_ref[0, t, head_rows[head], :]
            q_pad = jnp.concatenate(parts, axis=0)
            s_cur = jnp.dot(k_cur, q_pad, preferred_element_type=F32) + bias_ref[head, 0]
            s_prev = (jnp.dot(k_prev, q_pad, preferred_element_type=F32)
                      + (bias_ref[head, 1, half:, :] + prev_pen))
            bufs[slot][0][head] = s_cur
            bufs[slot][1][head] = s_prev
            m = jnp.maximum(jnp.max(s_cur, axis=0, keepdims=True), jnp.max(s_prev, axis=0, keepdims=True))
            maxes.append(jnp.maximum(m, sink_ref[head] * LOG2E))
        return tuple(maxes)

    def finish(slot, t, maxes):
        blk = base + t
        blk_prev = jnp.maximum(blk - 1, 0)
        outs = []
        for g in range(SWA_KV_HEADS):
            v_cur = jnp.concatenate([vt_ref[0, blk, head_rows[g], :], _ones_pad_rows(T)], axis=0)
            v_prev = jnp.concatenate([vt_ref[0, blk_prev, head_rows[g], half:], _ones_pad_rows(half)], axis=0)
            for head in range(g * SWA_GROUP, (g + 1) * SWA_GROUP):
                m = maxes[head]
                p_cur = jnp.exp2(bufs[slot][0][head] - m).astype(BF16)
                p_prev = jnp.exp2(bufs[slot][1][head] - m).astype(BF16)
                acc = (jnp.dot(v_cur, p_cur, preferred_element_type=F32)
                       + jnp.dot(v_prev, p_prev, preferred_element_type=F32))
                den = acc[HEAD_DIM:HEAD_DIM + 1] + jnp.exp2(sink_ref[head] * LOG2E - m)
                outs.append(acc[:HEAD_DIM] / den)
        o_ref[0, pl.ds(pl.multiple_of(t * T, T), T), :] = jnp.concatenate(outs, axis=0).T.astype(BF16)

    def block_group(c, maxes):
        for n in range(SWA_UNROLL):
            t = c * SWA_UNROLL + n
            nxt = score((n + 1) % 2, t + 1)
            finish(n % 2, t, maxes)
            maxes = nxt
        return maxes

    lax.fori_loop(0, per_step // SWA_UNROLL, block_group, score(0, jnp.int32(0)))


def _swa(sinks, qt_a, k_a, vt_a, swa_bias):
    B, nblk = qt_a.shape[0], qt_a.shape[1]
    S = nblk * T
    per_step = nblk // SWA_SPLIT
    assert nblk % SWA_SPLIT == 0 and per_step % SWA_UNROLL == 0 and SWA_UNROLL % 2 == 0
    return pl.pallas_call(
        functools.partial(_swa_kernel, per_step=per_step),
        grid=(B, SWA_SPLIT),
        in_specs=[
            pl.BlockSpec(memory_space=pltpu.SMEM),
            pl.BlockSpec((1, per_step, SWA_WIDTH, T), lambda b, h: (b, h, 0, 0)),
            pl.BlockSpec((1, S, SWA_KV_WIDTH), lambda b, h: (b, 0, 0)),
            pl.BlockSpec((1, nblk, SWA_KV_WIDTH, T), lambda b, h: (b, 0, 0, 0)),
            pl.BlockSpec((SWA_Q_HEADS, 2, T, T), lambda b, h: (0, 0, 0, 0)),
        ],
        out_specs=pl.BlockSpec((1, per_step * T, SWA_WIDTH), lambda b, h: (b, h, 0)),
        out_shape=jax.ShapeDtypeStruct((B, S, SWA_WIDTH), BF16),
        scratch_shapes=[pltpu.VMEM((SWA_Q_HEADS, T, T), F32),
                        pltpu.VMEM((SWA_Q_HEADS, T // 2, T), F32)] * 2,
        compiler_params=pltpu.CompilerParams(
            dimension_semantics=("arbitrary", "arbitrary"), vmem_limit_bytes=VMEM_LIMIT),
        name="swa_attention",
    )(sinks, qt_a, k_a, vt_a, swa_bias)


def _moba_kernel(qt_ref, k_ref, vt_ref, bias_ref, o_ref,
                 kaug_ref, vaug_ref, kmean_ref, qaug_ref, acc_ref,
                 s00_ref, s01_ref, s10_ref, s11_ref, *, nblk):
    s_refs = ((s00_ref, s01_ref), (s10_ref, s11_ref))
    heads = range(2)
    head_rows = [slice(hl * HEAD_DIM, (hl + 1) * HEAD_DIM) for hl in heads]

    rows = KEY_GROUP * T
    sel_r = lax.broadcasted_iota(jnp.int32, (2 * HEAD_DIM, AUG), 0)
    sel_c = lax.broadcasted_iota(jnp.int32, (2 * HEAD_DIM, AUG), 1)
    picks = [jnp.where((sel_r == sel_c + hl * HEAD_DIM) & (sel_c < HEAD_DIM), 1.0, 0.0).astype(BF16)
             for hl in heads]
    lane = lax.broadcasted_iota(jnp.int32, (rows, AUG), 1)
    row_blk = lax.shift_right_logical(lax.broadcasted_iota(jnp.int32, (rows, AUG), 0), T.bit_length() - 1)
    pad = _ones_pad_rows(T)

    def prep(c, carry):
        row0 = pl.multiple_of(c * rows, rows)
        blk0 = pl.multiple_of(c * KEY_GROUP, KEY_GROUP)
        kb = k_ref[0, pl.ds(row0, rows), :]
        onehot = jnp.where(lane == HEAD_DIM + blk0 + row_blk, 1.0, 0.0)
        for hl in heads:
            kp = jnp.dot(kb, picks[hl], preferred_element_type=F32)
            kaug_ref[hl, pl.ds(row0, rows), :] = (kp + onehot).astype(BF16)
            kmean_ref[hl, pl.ds(blk0, KEY_GROUP), :] = (
                jnp.sum(kp.reshape(KEY_GROUP, T, AUG), axis=1) * (1.0 / T))
            for r in range(KEY_GROUP):
                vaug_ref[hl, c * (KEY_GROUP // 2) + r // 2, :, (r % 2) * T:(r % 2 + 1) * T] = (
                    jnp.concatenate([vt_ref[0, blk0 + r, head_rows[hl], :], pad], axis=0))
        return carry

    lax.fori_loop(0, nblk // KEY_GROUP, prep, 0)

    width = QUERY_GROUP * T
    zeros_q = jnp.zeros((AUG - HEAD_DIM, T), BF16)
    kmeans = []
    for hl in heads:
        km = kmean_ref[hl]
        km_hi = km.astype(BF16)
        kmeans.append((km_hi, (km - km_hi.astype(F32)).astype(BF16)))

    for blk0 in range(0, nblk, QUERY_GROUP):
        rows_used = min(-(-(blk0 + QUERY_GROUP) // 16) * 16, nblk)
        blk_id = lax.broadcasted_iota(jnp.int32, (rows_used, width), 0)
        q_blk = blk0 + lax.shift_right_logical(
            lax.broadcasted_iota(jnp.int32, (rows_used, width), 1), T.bit_length() - 1)
        zeros_tail = jnp.zeros((AUG - HEAD_DIM - rows_used, T), BF16)
        for hl in heads:
            km_hi, km_lo = (km[:rows_used] for km in kmeans[hl])
            qts = [qt_ref[0, blk0 + x, head_rows[hl], :] for x in range(QUERY_GROUP)]
            scs = []
            for qt in qts:
                qs = jnp.concatenate([qt, zeros_q], axis=0)
                scs.append(jnp.dot(km_hi, qs, preferred_element_type=F32)
                           + jnp.dot(km_lo, qs, preferred_element_type=F32))
            sc = jnp.concatenate(scs, axis=1)
            sc = jnp.where(blk_id < q_blk, sc, -jnp.inf)
            chosen = blk_id == q_blk
            for _ in range(MOBA_TOPK):
                mx = jnp.max(sc, axis=0, keepdims=True)
                first = jnp.min(jnp.where(sc == mx, blk_id, nblk), axis=0, keepdims=True)
                hit = (blk_id == first) & (mx > -jnp.inf)
                chosen = chosen | hit
                sc = jnp.where(hit, -jnp.inf, sc)
            sel_bias = jnp.where(chosen, 0.0, NEG).astype(BF16)
            for x in range(QUERY_GROUP):
                qaug_ref[hl, blk0 + x] = jnp.concatenate(
                    [qts[x], sel_bias[:, x * T:(x + 1) * T], zeros_tail], axis=0)

    def next_item(i, u):
        last = u + 1 == (i + 2) // 2
        return jnp.where(last, i + 1, i), jnp.where(last, 0, u + 1)

    def score(hl, slot, i, u, m):
        i = jnp.minimum(i, nblk - 1)
        row0 = pl.multiple_of(u * 2 * T, 2 * T)
        q_aug = qaug_ref[hl, i]
        s_even = (jnp.dot(kaug_ref[hl, pl.ds(row0, T), :], q_aug, preferred_element_type=F32)
                  + bias_ref[hl, jnp.clip(i - 2 * u, 0, N_NEAR)])
        s_odd = (jnp.dot(kaug_ref[hl, pl.ds(row0 + T, T), :], q_aug, preferred_element_type=F32)
                 + bias_ref[hl, jnp.clip(i - 2 * u - 1, 0, N_NEAR)])
        s_refs[hl][slot][0:T, :] = s_even
        s_refs[hl][slot][T:, :] = s_odd
        m = jnp.where(u == 0, NEG, m)
        top = jnp.maximum(jnp.max(s_even, axis=0, keepdims=True), jnp.max(s_odd, axis=0, keepdims=True))
        m_new = jnp.maximum(m, top)
        return m_new, jnp.exp2(m - m_new)

    def accumulate(hl, slot, i, u, m, alpha):
        p = jnp.exp2(s_refs[hl][slot][...] - m).astype(BF16)
        acc_ref[hl, i] = acc_ref[hl, i] * alpha + jnp.dot(vaug_ref[hl, u], p, preferred_element_type=F32)

    def item_step(slot, i, u, state):
        i_nxt, u_nxt = next_item(i, u)
        nxt = tuple(score(hl, 1 - slot, i_nxt, u_nxt, state[hl][0]) for hl in heads)
        for hl in heads:
            accumulate(hl, slot, i, u, *state[hl])
        return i_nxt, u_nxt, nxt

    def item_group(_, carry):
        i, u, state = carry
        for n in range(ITEM_UNROLL):
            i, u, state = item_step(n % 2, i, u, state)
        return i, u, state

    acc_ref[...] = jnp.zeros(acc_ref.shape, F32)
    zero = jnp.int32(0)
    m0 = jnp.full((1, T), NEG, F32)
    state = tuple(score(hl, 0, zero, zero, m0) for hl in heads)
    n_items = sum((i + 2) // 2 for i in range(nblk))
    assert n_items % ITEM_UNROLL == 0 and ITEM_UNROLL % 2 == 0
    lax.fori_loop(0, n_items // ITEM_UNROLL, item_group, (zero, zero, state))

    def finish(c, carry):
        for x in range(QUERY_GROUP):
            i = c * QUERY_GROUP + x
            outs = []
            for hl in heads:
                acc = acc_ref[hl, i]
                outs.append(acc[:HEAD_DIM] / acc[HEAD_DIM:HEAD_DIM + 1])
            o_ref[0, pl.ds(pl.multiple_of(i * T, T), T), :] = jnp.concatenate(outs, axis=0).T.astype(BF16)
        return carry

    lax.fori_loop(0, nblk // QUERY_GROUP, finish, 0)


def _moba(qt_b, k_b, vt_b, moba_bias):
    B, nblk = qt_b.shape[0], qt_b.shape[1]
    S = nblk * T
    assert HEAD_DIM + nblk <= AUG, "block one-hot must fit the augmented contraction width"
    assert nblk % KEY_GROUP == 0 and nblk % QUERY_GROUP == 0 and nblk % 2 == 0
    pair = 2 * HEAD_DIM
    return pl.pallas_call(
        functools.partial(_moba_kernel, nblk=nblk),
        grid=(B, MOBA_HEADS // 2),
        in_specs=[
            pl.BlockSpec((1, nblk, pair, T), lambda b, h: (b, 0, h, 0)),
            pl.BlockSpec((1, S, pair), lambda b, h: (b, 0, h)),
            pl.BlockSpec((1, nblk, pair, T), lambda b, h: (b, 0, h, 0)),
            pl.BlockSpec((2, N_NEAR + 1, T, T), lambda b, h: (h, 0, 0, 0)),
        ],
        out_specs=pl.BlockSpec((1, S, pair), lambda b, h: (b, 0, h)),
        out_shape=jax.ShapeDtypeStruct((B, S, MOBA_WIDTH), BF16),
        scratch_shapes=[
            pltpu.VMEM((2, S, AUG), BF16),
            pltpu.VMEM((2, nblk // 2, V_ROWS, 2 * T), BF16),
            pltpu.VMEM((2, nblk, AUG), F32),
            pltpu.VMEM((2, nblk, AUG, T), BF16),
            pltpu.VMEM((2, nblk, V_ROWS, T), F32),
        ] + [pltpu.VMEM((2 * T, T), F32)] * 4,
        compiler_params=pltpu.CompilerParams(
            dimension_semantics=("arbitrary", "arbitrary"), vmem_limit_bytes=VMEM_LIMIT),
        name="moba_attention",
    )(qt_b, k_b, vt_b, moba_bias)


def _sigmoid(v):
    return 1.0 / (1.0 + jnp.exp(-v))


def _rms(v, gain):
    r = lax.rsqrt(jnp.mean(v * v, axis=-1, keepdims=True) + RMS_EPS)
    return (v * r) * gain


def _outproj_kernel(x_ref, ao_ref, bo_ref, ga_ref, gb_ref, p_ref,
                    wo_ref, pg_ref, wg_ref, wp_ref, fg_ref, o_ref, *, final):
    chunk = OUT_ROW_TILE // OUT_CHUNKS
    chunks = [slice(c * chunk, (c + 1) * chunk) for c in range(OUT_CHUNKS)]

    def mix(rows):
        ga = ga_ref[0, rows, :].astype(F32)
        gb = gb_ref[0, rows, :].astype(F32)
        ma = (ao_ref[0, rows, :].astype(F32) * (ga * _sigmoid(ga))).astype(BF16)
        mb = (bo_ref[0, rows, :].astype(F32) * (gb * _sigmoid(gb))).astype(BF16)
        return (x_ref[0, rows, :]
                + jnp.dot(ma, wo_ref[0:SWA_WIDTH, :], preferred_element_type=F32)
                + jnp.dot(mb, wo_ref[SWA_WIDTH:, :], preferred_element_type=F32))

    def gate_up(rows, x1):
        n = _rms(x1, pg_ref[...]).astype(BF16)
        gate = _sigmoid(jnp.dot(n, wg_ref[...], preferred_element_type=F32))
        up = jnp.dot(p_ref[0, rows, :].astype(BF16), wp_ref[...], preferred_element_type=F32)
        return x1 + up * gate

    x1s = [mix(rows) for rows in chunks]
    ys = [gate_up(rows, x1) for rows, x1 in zip(chunks, x1s)]
    for rows, y in zip(chunks, ys):
        o_ref[0, rows, :] = _rms(y, fg_ref[...]) if final else y


def _outproj(x, a_out, b_out, g_a, g_b, p, w_out, ple_gain, w_gate, w_proj, final_gain, final):
    B, S, D = x.shape
    nt = S // OUT_ROW_TILE
    ple = p.shape[-1]

    def tok(width):
        return pl.BlockSpec((1, OUT_ROW_TILE, width), lambda b, t: (b, t, 0))

    def whole(shape):
        return pl.BlockSpec(shape, lambda b, t: (0,) * len(shape))

    return pl.pallas_call(
        functools.partial(_outproj_kernel, final=final),
        grid=(B, nt),
        in_specs=[tok(D), tok(SWA_WIDTH), tok(MOBA_WIDTH), tok(SWA_WIDTH), tok(MOBA_WIDTH), tok(ple),
                  whole((D, D)), whole((1, D)), whole((D, D)), whole((ple, D)), whole((1, D))],
        out_specs=tok(D),
        out_shape=jax.ShapeDtypeStruct((B, S, D), F32),
        compiler_params=pltpu.CompilerParams(
            dimension_semantics=("arbitrary", "arbitrary"), vmem_limit_bytes=VMEM_LIMIT),
        name="outproj_ple",
    )(x, a_out, b_out, g_a, g_b, p, w_out, ple_gain, w_gate, w_proj, final_gain)


def _wprep_kernel(win_ref, wo_ref, wg_ref, wp_ref, wnn_ref, wt_ref, wob_ref, wgb_ref, wpb_ref):
    widths = (SWA_WIDTH, SWA_KV_WIDTH, SWA_KV_WIDTH, SWA_WIDTH,
              MOBA_WIDTH, MOBA_WIDTH, MOBA_WIDTH, MOBA_WIDTH)
    offs = [0]
    for wd in widths:
        offs.append(offs[-1] + wd)
    w = win_ref[...]
    a_q, a_k, a_v, a_g, b_q, b_k, b_v, b_g = (w[:, offs[n]:offs[n + 1]] for n in range(8))
    scale = HEAD_DIM ** -0.5 * LOG2E
    wnn_ref[...] = jnp.concatenate([a_g, b_k, b_g], axis=1).astype(BF16)
    wt_ref[...] = jnp.concatenate([a_k, a_q * scale, a_v, b_q * scale, b_v], axis=1).T.astype(BF16)
    wob_ref[...] = wo_ref[...].astype(BF16)
    wgb_ref[...] = wg_ref[...].astype(BF16)
    wpb_ref[...] = wp_ref[...].astype(BF16)


def _prepare_weights(w_in, w_out, w_gate, w_proj):
    D, n_in = w_in.shape
    ple = w_proj.shape[0]
    steps = D // W_ROWS
    n_nn = SWA_WIDTH + 2 * MOBA_WIDTH
    n_t = n_in - n_nn
    rows = lambda width: pl.BlockSpec((W_ROWS, width), lambda k: (k, 0))
    return pl.pallas_call(
        _wprep_kernel,
        grid=(steps,),
        in_specs=[rows(n_in), rows(D), rows(D), pl.BlockSpec((ple // steps, D), lambda k: (k, 0))],
        out_specs=[rows(n_nn), pl.BlockSpec((n_t, W_ROWS), lambda k: (0, k)), rows(D), rows(D),
                   pl.BlockSpec((ple // steps, D), lambda k: (k, 0))],
        out_shape=[jax.ShapeDtypeStruct((D, n_nn), BF16), jax.ShapeDtypeStruct((n_t, D), BF16),
                   jax.ShapeDtypeStruct((D, D), BF16), jax.ShapeDtypeStruct((D, D), BF16),
                   jax.ShapeDtypeStruct((ple, D), BF16)],
        name="weight_prep",
    )(w_in, w_out, w_gate, w_proj)


def kernel(x, p, norm_in, w_in, sinks, rel_bias, w_out, ple_norm, w_ple_gate, w_ple_proj, final_norm):
    B, S, D = x.shape
    depth = p.shape[0]
    assert S % ROW_TILE == 0 and ROW_TILE % T == 0 and S % OUT_ROW_TILE == 0
    moba_bias, swa_bias = _bias_tables(rel_bias)
    for i in range(depth):
        w_nn, w_t, w_o, w_g, w_p = _prepare_weights(w_in[i], w_out[i], w_ple_gate[i], w_ple_proj[i])
        k_a, g_a, k_b, g_b, qt_a, vt_a, qt_b, vt_b = _inproj(x, norm_in[i][None, :], w_nn, w_t)
        a_out = _swa(sinks[i], qt_a, k_a, vt_a, swa_bias)
        b_out = _moba(qt_b, k_b, vt_b, moba_bias)
        x = _outproj(x, a_out, b_out, g_a, g_b, p[i], w_o, ple_norm[i][None, :], w_g, w_p,
                     final_norm[None, :], final=(i == depth - 1))
    return x
```

```python
import functools
import math

import jax
import jax.numpy as jnp
from jax import lax
from jax.experimental import pallas as pl
from jax.experimental.pallas import tpu as pltpu

F32 = jnp.float32
BF16 = jnp.bfloat16

HEAD_DIM = 64
SWA_Q_HEADS = 8
SWA_KV_HEADS = 2
MOBA_HEADS = 8
WINDOW = 128
MOBA_BLOCK = 256
MOBA_TOPK = 3
NUM_BUCKETS = 32
MAX_DISTANCE = 1024
RMS_EPS = 1e-6

SWA_WIDTH = SWA_Q_HEADS * HEAD_DIM
SWA_KV_WIDTH = SWA_KV_HEADS * HEAD_DIM
MOBA_WIDTH = MOBA_HEADS * HEAD_DIM
SWA_GROUP = SWA_Q_HEADS // SWA_KV_HEADS

T = MOBA_BLOCK
AUG = 128
V_ROWS = HEAD_DIM + 16
NEG = -1e30
LOG2E = math.log2(math.e)
N_NEAR = -(-(MAX_DISTANCE + T) // T)
KEY_GROUP = 8
QUERY_GROUP = 4
ITEM_UNROLL = 34
assert WINDOW <= T // 2
SWA_SPLIT = 2
SWA_UNROLL = 8
ROW_TILE = 1024
W_ROWS = 128
OUT_ROW_TILE = 1024
OUT_CHUNKS = 8
VMEM_LIMIT = 56 * 1024 * 1024


def _bucket_starts():
    max_exact = NUM_BUCKETS // 2
    starts = list(range(max_exact))
    for j in range(NUM_BUCKETS - max_exact):
        edge = max_exact * (MAX_DISTANCE / max_exact) ** (j / (NUM_BUCKETS - max_exact))
        starts.append(math.ceil(edge - 1e-9))
    return starts


BUCKET_STARTS = _bucket_starts()


def _bias_of_distance(dist, table, dist_lo, dist_hi):
    bucket = lambda n: max(b for b in range(NUM_BUCKETS) if BUCKET_STARTS[b] <= max(n, 0))
    lo, hi = bucket(dist_lo), bucket(dist_hi)
    out = jnp.full(dist.shape, table(lo), F32)
    for b in range(lo + 1, hi + 1):
        out = jnp.where(dist >= BUCKET_STARTS[b], table(b), out)
    return out


def _bias_kernel(rel_ref, moba_ref, swa_ref):
    h = pl.program_id(0)
    kk = lax.broadcasted_iota(jnp.int32, (T, T), 0)
    qq = lax.broadcasted_iota(jnp.int32, (T, T), 1)
    rel = qq - kk

    def lookup(dist, col, dist_lo, dist_hi):
        return _bias_of_distance(dist, lambda b: rel_ref[b, col] * LOG2E, dist_lo, dist_hi)

    for d in range(N_NEAR):
        dist = rel + d * T
        bias = lookup(dist, SWA_Q_HEADS + h, max(d * T - (T - 1), 0), d * T + (T - 1))
        if d == 0:
            bias = jnp.where(dist >= 0, bias, NEG)
        moba_ref[0, d] = bias
    far = jnp.full((T, T), N_NEAR * T, jnp.int32)
    moba_ref[0, N_NEAR] = lookup(far, SWA_Q_HEADS + h, N_NEAR * T, N_NEAR * T)

    for d in range(2):
        dist = rel + d * T
        bias = lookup(dist, h, 0, WINDOW - 1)
        valid = (dist >= 0) & (dist < WINDOW)
        swa_ref[0, d] = jnp.where(valid, bias, NEG)


def _bias_tables(rel_bias):
    assert MOBA_HEADS == SWA_Q_HEADS
    return pl.pallas_call(
        _bias_kernel,
        grid=(MOBA_HEADS,),
        in_specs=[pl.BlockSpec(memory_space=pltpu.SMEM)],
        out_specs=[
            pl.BlockSpec((1, N_NEAR + 1, T, T), lambda h: (h, 0, 0, 0)),
            pl.BlockSpec((1, 2, T, T), lambda h: (h, 0, 0, 0)),
        ],
        out_shape=[
            jax.ShapeDtypeStruct((MOBA_HEADS, N_NEAR + 1, T, T), F32),
            jax.ShapeDtypeStruct((SWA_Q_HEADS, 2, T, T), F32),
        ],
        name="bias_tables",
    )(rel_bias)


def _inproj_kernel(x_ref, g_ref, wnn_ref, wt_ref,
                   ka_ref, ga_ref, kb_ref, gb_ref,
                   qta_ref, vta_ref, qtb_ref, vtb_ref):
    x = x_ref[0]
    r = lax.rsqrt(jnp.mean(x * x, axis=-1, keepdims=True) + RMS_EPS)
    h = ((x * r) * g_ref[...]).astype(BF16)

    tok = jnp.dot(h, wnn_ref[...], preferred_element_type=F32)
    o = 0
    for ref, width in ((ga_ref, SWA_WIDTH), (kb_ref, MOBA_WIDTH), (gb_ref, MOBA_WIDTH)):
        ref[0] = tok[:, o:o + width].astype(BF16)
        o += width

    feat = lax.dot_general(wt_ref[...], h, (((1,), (1,)), ((), ())),
                           preferred_element_type=F32)
    ka_ref[0] = feat[0:SWA_KV_WIDTH, :].T.astype(BF16)
    o = SWA_KV_WIDTH
    for ref, width in ((qta_ref, SWA_WIDTH), (vta_ref, SWA_KV_WIDTH),
                       (qtb_ref, MOBA_WIDTH), (vtb_ref, MOBA_WIDTH)):
        for c in range(ROW_TILE // T):
            ref[0, c] = feat[o:o + width, c * T:(c + 1) * T].astype(BF16)
        o += width


def _inproj(x, gain, w_nn, w_t):
    B, S, D = x.shape
    nt = S // ROW_TILE
    cpt = ROW_TILE // T
    n_nn = w_nn.shape[1]
    n_t = w_t.shape[0]

    def tok_spec(width):
        return pl.BlockSpec((1, ROW_TILE, width), lambda b, t: (b, t, 0))

    def feat_spec(width):
        return pl.BlockSpec((1, cpt, width, T), lambda b, t: (b, t, 0, 0))

    def tok_shape(width):
        return jax.ShapeDtypeStruct((B, S, width), BF16)

    def feat_shape(width):
        return jax.ShapeDtypeStruct((B, S // T, width, T), BF16)

    return pl.pallas_call(
        _inproj_kernel,
        grid=(B, nt),
        in_specs=[
            pl.BlockSpec((1, ROW_TILE, D), lambda b, t: (b, t, 0)),
            pl.BlockSpec((1, D), lambda b, t: (0, 0)),
            pl.BlockSpec((D, n_nn), lambda b, t: (0, 0)),
            pl.BlockSpec((n_t, D), lambda b, t: (0, 0)),
        ],
        out_specs=[tok_spec(SWA_KV_WIDTH), tok_spec(SWA_WIDTH), tok_spec(MOBA_WIDTH), tok_spec(MOBA_WIDTH),
                   feat_spec(SWA_WIDTH), feat_spec(SWA_KV_WIDTH), feat_spec(MOBA_WIDTH), feat_spec(MOBA_WIDTH)],
        out_shape=[tok_shape(SWA_KV_WIDTH), tok_shape(SWA_WIDTH), tok_shape(MOBA_WIDTH), tok_shape(MOBA_WIDTH),
                   feat_shape(SWA_WIDTH), feat_shape(SWA_KV_WIDTH), feat_shape(MOBA_WIDTH), feat_shape(MOBA_WIDTH)],
        compiler_params=pltpu.CompilerParams(
            dimension_semantics=("arbitrary", "arbitrary"), vmem_limit_bytes=VMEM_LIMIT),
        name="norm_inproj",
    )(x, gain, w_nn, w_t)


def _ones_pad_rows(width):
    row = lax.broadcasted_iota(jnp.int32, (V_ROWS - HEAD_DIM, width), 0)
    return jnp.where(row == 0, 1.0, 0.0).astype(BF16)


def _swa_kernel(sink_ref, qt_ref, k_ref, vt_ref, bias_ref, o_ref, buf0_ref, buf1_ref, *, per_step):
    bufs = (buf0_ref, buf1_ref)
    half = T // 2
    base = pl.program_id(1) * per_step
    zeros_q = jnp.zeros((HEAD_DIM, half), BF16)
    head_rows = [slice(h * HEAD_DIM, (h + 1) * HEAD_DIM) for h in range(SWA_Q_HEADS)]
    lane = lax.broadcasted_iota(jnp.int32, (1, T), 1)
    tiles = [(g, g * SWA_GROUP + 2 * pr, hf)
             for g in range(SWA_KV_HEADS) for pr in range(SWA_GROUP // 2) for hf in range(2)]

    def sink_lanes(ha):
        return jnp.where(lane < half, sink_ref[ha], sink_ref[ha + 1]) * LOG2E

    def score(slot, t):
        t = jnp.minimum(t, per_step - 1)
        blk = base + t
        row0 = pl.multiple_of(blk * T, T)
        k_cur = k_ref[0, pl.ds(row0, T), :]
        k_prev = k_ref[0, pl.ds(pl.multiple_of(jnp.maximum(row0 - half, 0), half), half), :]
        k_low = jnp.concatenate([k_prev, k_cur[:half]], axis=0)
        prev_pen = jnp.where(blk == 0, NEG, 0.0).astype(F32)
        maxes = []
        for n, (g, ha, hf) in enumerate(tiles):
            q_lanes = slice(hf * half, (hf + 1) * half)
            rhs, bias = [], []
            for h in (ha, ha + 1):
                parts = [zeros_q] * SWA_KV_HEADS
                parts[g] = qt_ref[0, t, head_rows[h], q_lanes]
                rhs.append(jnp.concatenate(parts, axis=0))
                if hf == 0:
                    bias.append(jnp.concatenate([bias_ref[h, 1, half:, 0:half] + prev_pen,
                                                 bias_ref[h, 0, 0:half, 0:half]], axis=0))
                else:
                    bias.append(bias_ref[h, 0, :, half:])
            keys = k_low if hf == 0 else k_cur
            s = (jnp.dot(keys, jnp.concatenate(rhs, axis=1), preferred_element_type=F32)
                 + jnp.concatenate(bias, axis=1))
            bufs[slot][n] = s
            maxes.append(jnp.maximum(jnp.max(s, axis=0, keepdims=True), sink_lanes(ha)))
        return tuple(maxes)

    def finish(slot, t, maxes):
        blk = base + t
        blk_prev = jnp.maximum(blk - 1, 0)
        pad = _ones_pad_rows(T)
        pieces = {}
        for n, (g, ha, hf) in enumerate(tiles):
            if hf == 0:
                v = jnp.concatenate([vt_ref[0, blk_prev, head_rows[g], half:],
                                     vt_ref[0, blk, head_rows[g], 0:half]], axis=1)
            else:
                v = vt_ref[0, blk, head_rows[g], :]
            m = maxes[n]
            p = jnp.exp2(bufs[slot][n] - m).astype(BF16)
            acc = jnp.dot(jnp.concatenate([v, pad], axis=0), p, preferred_element_type=F32)
            den = acc[HEAD_DIM:HEAD_DIM + 1] + jnp.exp2(sink_lanes(ha) - m)
            out = acc[:HEAD_DIM] / den
            pieces[(ha, hf)] = out[:, 0:half]
            pieces[(ha + 1, hf)] = out[:, half:]
        outs = [jnp.concatenate([pieces[(h, 0)], pieces[(h, 1)]], axis=1) for h in range(SWA_Q_HEADS)]
        o_ref[0, pl.ds(pl.multiple_of(t * T, T), T), :] = jnp.concatenate(outs, axis=0).T.astype(BF16)

    def block_group(c, maxes):
        for n in range(SWA_UNROLL):
            t = c * SWA_UNROLL + n
            nxt = score((n + 1) % 2, t + 1)
            finish(n % 2, t, maxes)
            maxes = nxt
        return maxes

    lax.fori_loop(0, per_step // SWA_UNROLL, block_group, score(0, jnp.int32(0)))


def _swa(sinks, qt_a, k_a, vt_a, swa_bias):
    B, nblk = qt_a.shape[0], qt_a.shape[1]
    S = nblk * T
    per_step = nblk // SWA_SPLIT
    assert nblk % SWA_SPLIT == 0 and per_step % SWA_UNROLL == 0 and SWA_UNROLL % 2 == 0
    return pl.pallas_call(
        functools.partial(_swa_kernel, per_step=per_step),
        grid=(B, SWA_SPLIT),
        in_specs=[
            pl.BlockSpec(memory_space=pltpu.SMEM),
            pl.BlockSpec((1, per_step, SWA_WIDTH, T), lambda b, h: (b, h, 0, 0)),
            pl.BlockSpec((1, S, SWA_KV_WIDTH), lambda b, h: (b, 0, 0)),
            pl.BlockSpec((1, nblk, SWA_KV_WIDTH, T), lambda b, h: (b, 0, 0, 0)),
            pl.BlockSpec((SWA_Q_HEADS, 2, T, T), lambda b, h: (0, 0, 0, 0)),
        ],
        out_specs=pl.BlockSpec((1, per_step * T, SWA_WIDTH), lambda b, h: (b, h, 0)),
        out_shape=jax.ShapeDtypeStruct((B, S, SWA_WIDTH), BF16),
        scratch_shapes=[pltpu.VMEM((SWA_Q_HEADS, T, T), F32)] * 2,
        compiler_params=pltpu.CompilerParams(
            dimension_semantics=("arbitrary", "arbitrary"), vmem_limit_bytes=VMEM_LIMIT),
        name="swa_attention",
    )(sinks, qt_a, k_a, vt_a, swa_bias)


def _moba_kernel(qt_ref, k_ref, vt_ref, bias_ref, o_ref,
                 kaug_ref, vaug_ref, kmean_ref, qaug_ref, acc_ref,
                 s00_ref, s01_ref, s10_ref, s11_ref, *, nblk):
    s_refs = ((s00_ref, s01_ref), (s10_ref, s11_ref))
    heads = range(2)
    head_rows = [slice(hl * HEAD_DIM, (hl + 1) * HEAD_DIM) for hl in heads]

    rows = KEY_GROUP * T
    sel_r = lax.broadcasted_iota(jnp.int32, (2 * HEAD_DIM, AUG), 0)
    sel_c = lax.broadcasted_iota(jnp.int32, (2 * HEAD_DIM, AUG), 1)
    picks = [jnp.where((sel_r == sel_c + hl * HEAD_DIM) & (sel_c < HEAD_DIM), 1.0, 0.0).astype(BF16)
             for hl in heads]
    lane = lax.broadcasted_iota(jnp.int32, (rows, AUG), 1)
    row_blk = lax.shift_right_logical(lax.broadcasted_iota(jnp.int32, (rows, AUG), 0), T.bit_length() - 1)
    pad = _ones_pad_rows(T)

    def prep(c, carry):
        row0 = pl.multiple_of(c * rows, rows)
        blk0 = pl.multiple_of(c * KEY_GROUP, KEY_GROUP)
        kb = k_ref[0, pl.ds(row0, rows), :]
        onehot = jnp.where(lane == HEAD_DIM + blk0 + row_blk, 1.0, 0.0)
        for hl in heads:
            kp = jnp.dot(kb, picks[hl], preferred_element_type=F32)
            kaug_ref[hl, pl.ds(row0, rows), :] = (kp + onehot).astype(BF16)
            kmean_ref[hl, pl.ds(blk0, KEY_GROUP), :] = (
                jnp.sum(kp.reshape(KEY_GROUP, T, AUG), axis=1) * (1.0 / T))
            for r in range(KEY_GROUP):
                vaug_ref[hl, c * (KEY_GROUP // 2) + r // 2, :, (r % 2) * T:(r % 2 + 1) * T] = (
                    jnp.concatenate([vt_ref[0, blk0 + r, head_rows[hl], :], pad], axis=0))
        return carry

    lax.fori_loop(0, nblk // KEY_GROUP, prep, 0)

    width = QUERY_GROUP * T
    zeros_q = jnp.zeros((AUG - HEAD_DIM, T), BF16)
    kmeans = []
    for hl in heads:
        km = kmean_ref[hl]
        km_hi = km.astype(BF16)
        kmeans.append((km_hi, (km - km_hi.astype(F32)).astype(BF16)))

    for blk0 in range(0, nblk, QUERY_GROUP):
        rows_used = min(-(-(blk0 + QUERY_GROUP) // 16) * 16, nblk)
        blk_id = lax.broadcasted_iota(jnp.int32, (rows_used, width), 0)
        q_blk = blk0 + lax.shift_right_logical(
            lax.broadcasted_iota(jnp.int32, (rows_used, width), 1), T.bit_length() - 1)
        zeros_tail = jnp.zeros((AUG - HEAD_DIM - rows_used, T), BF16)
        for hl in heads:
            km_hi, km_lo = (km[:rows_used] for km in kmeans[hl])
            qts = [qt_ref[0, blk0 + x, head_rows[hl], :] for x in range(QUERY_GROUP)]
            scs = []
            for qt in qts:
                qs = jnp.concatenate([qt, zeros_q], axis=0)
                scs.append(jnp.dot(km_hi, qs, preferred_element_type=F32)
                           + jnp.dot(km_lo, qs, preferred_element_type=F32))
            sc = jnp.concatenate(scs, axis=1)
            sc = jnp.where(blk_id < q_blk, sc, -jnp.inf)
            chosen = blk_id == q_blk
            for _ in range(MOBA_TOPK):
                mx = jnp.max(sc, axis=0, keepdims=True)
                first = jnp.min(jnp.where(sc == mx, blk_id, nblk), axis=0, keepdims=True)
                hit = (blk_id == first) & (mx > -jnp.inf)
                chosen = chosen | hit
                sc = jnp.where(hit, -jnp.inf, sc)
            sel_bias = jnp.where(chosen, 0.0, NEG).astype(BF16)
            for x in range(QUERY_GROUP):
                qaug_ref[hl, blk0 + x] = jnp.concatenate(
                    [qts[x], sel_bias[:, x * T:(x + 1) * T], zeros_tail], axis=0)

    def next_item(i, u):
        last = u + 1 == (i + 2) // 2
        return jnp.where(last, i + 1, i), jnp.where(last, 0, u + 1)

    def score(hl, slot, i, u, m):
        i = jnp.minimum(i, nblk - 1)
        row0 = pl.multiple_of(u * 2 * T, 2 * T)
        q_aug = qaug_ref[hl, i]
        s_even = (jnp.dot(kaug_ref[hl, pl.ds(row0, T), :], q_aug, preferred_element_type=F32)
                  + bias_ref[hl, jnp.clip(i - 2 * u, 0, N_NEAR)])
        s_odd = (jnp.dot(kaug_ref[hl, pl.ds(row0 + T, T), :], q_aug, preferred_element_type=F32)
                 + bias_ref[hl, jnp.clip(i - 2 * u - 1, 0, N_NEAR)])
        s_refs[hl][slot][0:T, :] = s_even
        s_refs[hl][slot][T:, :] = s_odd
        m = jnp.where(u == 0, NEG, m)
        top = jnp.maximum(jnp.max(s_even, axis=0, keepdims=True), jnp.max(s_odd, axis=0, keepdims=True))
        m_new = jnp.maximum(m, top)
        return m_new, jnp.exp2(m - m_new)

    def accumulate(hl, slot, i, u, m, alpha):
        p = jnp.exp2(s_refs[hl][slot][...] - m).astype(BF16)
        acc_ref[hl, i] = acc_ref[hl, i] * alpha + jnp.dot(vaug_ref[hl, u], p, preferred_element_type=F32)

    def item_step(slot, i, u, state):
        i_nxt, u_nxt = next_item(i, u)
        nxt = tuple(score(hl, 1 - slot, i_nxt, u_nxt, state[hl][0]) for hl in heads)
        for hl in heads:
            accumulate(hl, slot, i, u, *state[hl])
        return i_nxt, u_nxt, nxt

    def item_group(_, carry):
        i, u, state = carry
        for n in range(ITEM_UNROLL):
            i, u, state = item_step(n % 2, i, u, state)
        return i, u, state

    acc_ref[...] = jnp.zeros(acc_ref.shape, F32)
    zero = jnp.int32(0)
    m0 = jnp.full((1, T), NEG, F32)
    state = tuple(score(hl, 0, zero, zero, m0) for hl in heads)
    n_items = sum((i + 2) // 2 for i in range(nblk))
    assert n_items % ITEM_UNROLL == 0 and ITEM_UNROLL % 2 == 0
    lax.fori_loop(0, n_items // ITEM_UNROLL, item_group, (zero, zero, state))

    def finish(c, carry):
        for x in range(QUERY_GROUP):
            i = c * QUERY_GROUP + x
            outs = []
            for hl in heads:
                acc = acc_ref[hl, i]
                outs.append(acc[:HEAD_DIM] / acc[HEAD_DIM:HEAD_DIM + 1])
            o_ref[0, pl.ds(pl.multiple_of(i * T, T), T), :] = jnp.concatenate(outs, axis=0).T.astype(BF16)
        return carry

    lax.fori_loop(0, nblk // QUERY_GROUP, finish, 0)


def _moba(qt_b, k_b, vt_b, moba_bias):
    B, nblk = qt_b.shape[0], qt_b.shape[1]
    S = nblk * T
    assert HEAD_DIM + nblk <= AUG, "block one-hot must fit the augmented contraction width"
    assert nblk % KEY_GROUP == 0 and nblk % QUERY_GROUP == 0 and nblk % 2 == 0
    pair = 2 * HEAD_DIM
    return pl.pallas_call(
        functools.partial(_moba_kernel, nblk=nblk),
        grid=(B, MOBA_HEADS // 2),
        in_specs=[
            pl.BlockSpec((1, nblk, pair, T), lambda b, h: (b, 0, h, 0)),
            pl.BlockSpec((1, S, pair), lambda b, h: (b, 0, h)),
            pl.BlockSpec((1, nblk, pair, T), lambda b, h: (b, 0, h, 0)),
            pl.BlockSpec((2, N_NEAR + 1, T, T), lambda b, h: (h, 0, 0, 0)),
        ],
        out_specs=pl.BlockSpec((1, S, pair), lambda b, h: (b, 0, h)),
        out_shape=jax.ShapeDtypeStruct((B, S, MOBA_WIDTH), BF16),
        scratch_shapes=[
            pltpu.VMEM((2, S, AUG), BF16),
            pltpu.VMEM((2, nblk // 2, V_ROWS, 2 * T), BF16),
            pltpu.VMEM((2, nblk, AUG), F32),
            pltpu.VMEM((2, nblk, AUG, T), BF16),
            pltpu.VMEM((2, nblk, V_ROWS, T), F32),
        ] + [pltpu.VMEM((2 * T, T), F32)] * 4,
        compiler_params=pltpu.CompilerParams(
            dimension_semantics=("arbitrary", "arbitrary"), vmem_limit_bytes=VMEM_LIMIT),
        name="moba_attention",
    )(qt_b, k_b, vt_b, moba_bias)


def _sigmoid(v):
    return 1.0 / (1.0 + jnp.exp(-v))


def _rms(v, gain):
    r = lax.rsqrt(jnp.mean(v * v, axis=-1, keepdims=True) + RMS_EPS)
    return (v * r) * gain


def _outproj_kernel(x_ref, ao_ref, bo_ref, ga_ref, gb_ref, p_ref,
                    wo_ref, pg_ref, wg_ref, wp_ref, fg_ref, o_ref, *, final):
    chunk = OUT_ROW_TILE // OUT_CHUNKS
    chunks = [slice(c * chunk, (c + 1) * chunk) for c in range(OUT_CHUNKS)]

    def mix(rows):
        ga = ga_ref[0, rows, :].astype(F32)
        gb = gb_ref[0, rows, :].astype(F32)
        ma = (ao_ref[0, rows, :].astype(F32) * (ga * _sigmoid(ga))).astype(BF16)
        mb = (bo_ref[0, rows, :].astype(F32) * (gb * _sigmoid(gb))).astype(BF16)
        return (x_ref[0, rows, :]
                + jnp.dot(ma, wo_ref[0:SWA_WIDTH, :], preferred_element_type=F32)
                + jnp.dot(mb, wo_ref[SWA_WIDTH:, :], preferred_element_type=F32))

    def gate_up(rows, x1):
        n = _rms(x1, pg_ref[...]).astype(BF16)
        gate = _sigmoid(jnp.dot(n, wg_ref[...], preferred_element_type=F32))
        up = jnp.dot(p_ref[0, rows, :].astype(BF16), wp_ref[...], preferred_element_type=F32)
        return x1 + up * gate

    x1s = [mix(rows) for rows in chunks]
    ys = [gate_up(rows, x1) for rows, x1 in zip(chunks, x1s)]
    for rows, y in zip(chunks, ys):
        o_ref[0, rows, :] = _rms(y, fg_ref[...]) if final else y


def _outproj(x, a_out, b_out, g_a, g_b, p, w_out, ple_gain, w_gate, w_proj, final_gain, final):
    B, S, D = x.shape
    nt = S // OUT_ROW_TILE
    ple = p.shape[-1]

    def tok(width):
        return pl.BlockSpec((1, OUT_ROW_TILE, width), lambda b, t: (b, t, 0))

    def whole(shape):
        return pl.BlockSpec(shape, lambda b, t: (0,) * len(shape))

    return pl.pallas_call(
        functools.partial(_outproj_kernel, final=final),
        grid=(B, nt),
        in_specs=[tok(D), tok(SWA_WIDTH), tok(MOBA_WIDTH), tok(SWA_WIDTH), tok(MOBA_WIDTH), tok(ple),
                  whole((D, D)), whole((1, D)), whole((D, D)), whole((ple, D)), whole((1, D))],
        out_specs=tok(D),
        out_shape=jax.ShapeDtypeStruct((B, S, D), F32),
        compiler_params=pltpu.CompilerParams(
            dimension_semantics=("arbitrary", "arbitrary"), vmem_limit_bytes=VMEM_LIMIT),
        name="outproj_ple",
    )(x, a_out, b_out, g_a, g_b, p, w_out, ple_gain, w_gate, w_proj, final_gain)


def _wprep_kernel(win_ref, wo_ref, wg_ref, wp_ref, wnn_ref, wt_ref, wob_ref, wgb_ref, wpb_ref):
    widths = (SWA_WIDTH, SWA_KV_WIDTH, SWA_KV_WIDTH, SWA_WIDTH,
              MOBA_WIDTH, MOBA_WIDTH, MOBA_WIDTH, MOBA_WIDTH)
    offs = [0]
    for wd in widths:
        offs.append(offs[-1] + wd)
    w = win_ref[...]
    a_q, a_k, a_v, a_g, b_q, b_k, b_v, b_g = (w[:, offs[n]:offs[n + 1]] for n in range(8))
    scale = HEAD_DIM ** -0.5 * LOG2E
    wnn_ref[...] = jnp.concatenate([a_g, b_k, b_g], axis=1).astype(BF16)
    wt_ref[...] = jnp.concatenate([a_k, a_q * scale, a_v, b_q * scale, b_v], axis=1).T.astype(BF16)
    wob_ref[...] = wo_ref[...].astype(BF16)
    wgb_ref[...] = wg_ref[...].astype(BF16)
    wpb_ref[...] = wp_ref[...].astype(BF16)


def _prepare_weights(w_in, w_out, w_gate, w_proj):
    D, n_in = w_in.shape
    ple = w_proj.shape[0]
    steps = D // W_ROWS
    n_nn = SWA_WIDTH + 2 * MOBA_WIDTH
    n_t = n_in - n_nn
    rows = lambda width: pl.BlockSpec((W_ROWS, width), lambda k: (k, 0))
    return pl.pallas_call(
        _wprep_kernel,
        grid=(steps,),
        in_specs=[rows(n_in), rows(D), rows(D), pl.BlockSpec((ple // steps, D), lambda k: (k, 0))],
        out_specs=[rows(n_nn), pl.BlockSpec((n_t, W_ROWS), lambda k: (0, k)), rows(D), rows(D),
                   pl.BlockSpec((ple // steps, D), lambda k: (k, 0))],
        out_shape=[jax.ShapeDtypeStruct((D, n_nn), BF16), jax.ShapeDtypeStruct((n_t, D), BF16),
                   jax.ShapeDtypeStruct((D, D), BF16), jax.ShapeDtypeStruct((D, D), BF16),
                   jax.ShapeDtypeStruct((ple, D), BF16)],
        name="weight_prep",
    )(w_in, w_out, w_gate, w_proj)


def kernel(x, p, norm_in, w_in, sinks, rel_bias, w_out, ple_norm, w_ple_gate, w_ple_proj, final_norm):
    B, S, D = x.shape
    depth = p.shape[0]
    assert S % ROW_TILE == 0 and ROW_TILE % T == 0 and S % OUT_ROW_TILE == 0
    moba_bias, swa_bias = _bias_tables(rel_bias)
    for i in range(depth):
        w_nn, w_t, w_o, w_g, w_p = _prepare_weights(w_in[i], w_out[i], w_ple_gate[i], w_ple_proj[i])
        k_a, g_a, k_b, g_b, qt_a, vt_a, qt_b, vt_b = _inproj(x, norm_in[i][None, :], w_nn, w_t)
        a_out = _swa(sinks[i], qt_a, k_a, vt_a, swa_bias)
        b_out = _moba(qt_b, k_b, vt_b, moba_bias)
        x = _outproj(x, a_out, b_out, g_a, g_b, p[i], w_o, ple_norm[i][None, :], w_g, w_p,
                     final_norm[None, :], final=(i == depth - 1))
    return x
```
